```python
import jax, jax.numpy as jnp
from jax import lax
import numpy as np

D_MODEL = 4096
BATCH = 16
SEQ = 256
DEPTH = 2
DEC_BATCH = 8
DEC_SEQ = 1024
PAST_LEN = 512

GRID_W = 64
HEAD_DIM = 128
N_HEADS = D_MODEL // HEAD_DIM
POOL_WIDTH = D_MODEL // 2
CONV_WIDTH = D_MODEL // 2
POOL_WINDOWS = (2, 4, 8, 16)
N_POOL_GROUPS = len(POOL_WINDOWS)
POOL_GROUP = POOL_WIDTH // N_POOL_GROUPS
CONV_K = 31
WIN_ROWS_MAX = 8
WIN_COLS = 16
Q_COLS = 16
K_COLS = 32
CTX_QBLOCK = 128
D_FF = -(-8 * D_MODEL // (3 * 256)) * 256
N_EVEN = (DEPTH + 1) // 2
N_ODD = DEPTH // 2
RMS_EPS = 1e-6
LN_EPS = 1e-5

kernel_name = 'hybrid_pool_conv_natten_dit_step'


def rmsnorm(x, g):
    xf = x.astype(jnp.float32)
    y = xf * lax.rsqrt(jnp.mean(xf * xf, axis=-1, keepdims=True) + RMS_EPS)
    return (y * g.astype(jnp.float32)).astype(x.dtype)


def layernorm(x, g, b):
    xf = x.astype(jnp.float32)
    mu = jnp.mean(xf, axis=-1, keepdims=True)
    var = jnp.mean(jnp.square(xf - mu), axis=-1, keepdims=True)
    y = (xf - mu) * lax.rsqrt(var + LN_EPS)
    return (y * g.astype(jnp.float32) + b.astype(jnp.float32)).astype(x.dtype)


def ada_mod(cond, w, b):
    m = (jax.nn.silu(cond) @ w + b).reshape(cond.shape[0], 1, 6, D_MODEL)
    return tuple(m[:, :, j] for j in range(6))


def pool_mix(u, w_pool, scale):
    bsz, L, _ = u.shape
    t = jnp.arange(L)[:, None]
    lo = np.array([w // 2 for w in POOL_WINDOWS])
    hi = np.array([w - w // 2 - 1 for w in POOL_WINDOWS])
    idx_lo = jnp.maximum(t - lo, 0)
    idx_hi = jnp.minimum(t + hi + 1, L)
    count = (idx_hi - idx_lo).astype(jnp.float32)
    ug = u.reshape(bsz, L, N_POOL_GROUPS, POOL_GROUP).astype(jnp.float32)
    cs = jnp.concatenate([jnp.zeros_like(ug[:, :1]), jnp.cumsum(ug, axis=1)], axis=1)
    gi = jnp.arange(N_POOL_GROUPS)[None, :]
    pooled = (cs[:, idx_hi, gi] - cs[:, idx_lo, gi]) / count[None, :, :, None]
    mixed = (pooled - ug).astype(u.dtype)
    y = jnp.einsum('blgc,gcd->blgd', mixed, w_pool).reshape(bsz, L, POOL_WIDTH)
    return y * scale


def conv_module(u, w_dw, ln_g, ln_b):
    a, g = u[..., :CONV_WIDTH], u[..., CONV_WIDTH:]
    glu = a * jax.nn.sigmoid(g)
    y = lax.conv_general_dilated(
        glu, w_dw[:, None, :].astype(glu.dtype), window_strides=(1,),
        padding=[(CONV_K // 2, CONV_K // 2)], dimension_numbers=('NWC', 'WIO', 'NWC'),
        feature_group_count=CONV_WIDTH)
    return jax.nn.silu(layernorm(y, ln_g, ln_b))


def even_mixer(h, w_in, w_pool, pool_scale, w_dw, ln_g, ln_b, w_out):
    z = h @ w_in
    y_pool = pool_mix(z[..., :POOL_WIDTH], w_pool, pool_scale)
    y_conv = conv_module(z[..., POOL_WIDTH:], w_dw, ln_g, ln_b)
    return jnp.concatenate([y_pool, y_conv], axis=-1) @ w_out


def qkv_heads(h, w_qkv, q_g, k_g):
    bsz, L, _ = h.shape
    z = (h @ w_qkv).reshape(bsz, L, 3, N_HEADS, HEAD_DIM)
    return rmsnorm(z[:, :, 0], q_g), rmsnorm(z[:, :, 1], k_g), z[:, :, 2]


def context_attention(q, k, v):
    bsz, L = q.shape[0], q.shape[1]
    nb = L // CTX_QBLOCK
    scale = HEAD_DIM ** -0.5
    qb = q.reshape(bsz, nb, CTX_QBLOCK, N_HEADS, HEAD_DIM).transpose(1, 0, 2, 3, 4)

    def block(qi):
        s = jnp.einsum('bqhd,bkhd->bhqk', qi, k).astype(jnp.float32) * scale
        p = jax.nn.softmax(s, axis=-1).astype(v.dtype)
        return jnp.einsum('bhqk,bkhd->bqhd', p, v)

    o = lax.map(block, qb)
    return o.transpose(1, 0, 2, 3, 4).reshape(bsz, L, D_MODEL)


def neighborhood_attention(q, k, v, k_ctx, v_ctx, rpb):
    bsz, L = q.shape[0], q.shape[1]
    rows = L // GRID_W
    wr = min(WIN_ROWS_MAX, rows)
    ncb = GRID_W // Q_COLS
    scale = HEAD_DIM ** -0.5
    qc = np.arange(GRID_W).reshape(ncb, Q_COLS)
    cstart = np.clip(qc - WIN_COLS // 2, 0, GRID_W - WIN_COLS)
    kb = np.minimum(cstart[:, 0], GRID_W - K_COLS)
    kc = kb[:, None] + np.arange(K_COLS)
    col_ok = (kc[:, None, :] >= cstart[:, :, None]) & (kc[:, None, :] < cstart[:, :, None] + WIN_COLS)
    dc_idx = np.clip(kc[:, None, :] - qc[:, :, None], -(WIN_COLS - 1), WIN_COLS - 1) + WIN_COLS - 1
    mask = np.broadcast_to(col_ok[:, :, None, :], (ncb, Q_COLS, wr, K_COLS)).reshape(ncb, Q_COLS, wr * K_COLS)
    nloc = wr * K_COLS

    def row_fn(r):
        rs = jnp.clip(r - wr // 2, 0, rows - wr)
        q_r = lax.dynamic_slice_in_dim(q, r * GRID_W, GRID_W, axis=1).reshape(bsz, ncb, Q_COLS, N_HEADS, HEAD_DIM)
        k_r = lax.dynamic_slice_in_dim(k, rs * GRID_W, wr * GRID_W, axis=1).reshape(bsz, wr, GRID_W, N_HEADS, HEAD_DIM)
        v_r = lax.dynamic_slice_in_dim(v, rs * GRID_W, wr * GRID_W, axis=1).reshape(bsz, wr, GRID_W, N_HEADS, HEAD_DIM)
        k_blk = k_r[:, :, kc].transpose(0, 2, 1, 3, 4, 5).reshape(bsz, ncb, nloc, N_HEADS, HEAD_DIM)
        v_blk = v_r[:, :, kc].transpose(0, 2, 1, 3, 4, 5).reshape(bsz, ncb, nloc, N_HEADS, HEAD_DIM)
        dr = rs + jnp.arange(wr) - r + WIN_ROWS_MAX - 1
        bias = rpb[:, dr[:, None, None, None], dc_idx[None]]
        bias = bias.transpose(0, 2, 3, 1, 4).reshape(N_HEADS, ncb, Q_COLS, nloc).astype(jnp.float32)
        s_loc = jnp.einsum('bjqhd,bjkhd->bhjqk', q_r, k_blk).astype(jnp.float32) * scale + bias[None]
        s_loc = jnp.where(mask[None, None], s_loc, -jnp.inf)
        s_ctx = jnp.einsum('bjqhd,bkhd->bhjqk', q_r, k_ctx).astype(jnp.float32) * scale
        p = jax.nn.softmax(jnp.concatenate([s_loc, s_ctx], axis=-1), axis=-1).astype(v.dtype)
        o = (jnp.einsum('bhjqk,bjkhd->bjqhd', p[..., :nloc], v_blk)
             + jnp.einsum('bhjqk,bkhd->bjqhd', p[..., nloc:], v_ctx))
        return o.reshape(bsz, GRID_W, D_MODEL)

    out = lax.map(row_fn, jnp.arange(rows))
    return out.transpose(1, 0, 2, 3).reshape(bsz, L, D_MODEL)


def swiglu(h, w1, w3, w2):
    return (jax.nn.silu(h @ w1) * (h @ w3)) @ w2


def setup_inputs(seed: int = 0) -> dict:
    key = jax.random.key(seed)
    ks = jax.random.split(key, 26)

    def nrm(k, shape, s):
        return jax.random.normal(k, shape, jnp.float32) * s

    return {
        'x_prompt': nrm(ks[0], (BATCH, SEQ, D_MODEL), 1.0),
        'x_sample': nrm(ks[1], (DEC_BATCH, DEC_SEQ, D_MODEL), 1.0),
        'cache_k_na': nrm(ks[2], (DEC_BATCH, N_ODD, PAST_LEN, N_HEADS, HEAD_DIM), 1.0),
        'cache_v_na': nrm(ks[3], (DEC_BATCH, N_ODD, PAST_LEN, N_HEADS, HEAD_DIM), 1.0),
        'c': nrm(ks[4], (DEC_BATCH, D_MODEL), 1.0),
        'c_ctx': nrm(ks[5], (D_MODEL,), 1.0),
        'ada_w': nrm(ks[6], (DEPTH, D_MODEL, 6 * D_MODEL), 0.5 * D_MODEL ** -0.5),
        'ada_b': nrm(ks[7], (DEPTH, 6 * D_MODEL), 0.01),
        'norm1_g': 1.0 + nrm(ks[8], (DEPTH, D_MODEL), 0.1),
        'norm2_g': 1.0 + nrm(ks[9], (DEPTH, D_MODEL), 0.1),
        'ffn_w1': nrm(ks[10], (DEPTH, D_MODEL, D_FF), D_MODEL ** -0.5),
        'ffn_w3': nrm(ks[11], (DEPTH, D_MODEL, D_FF), D_MODEL ** -0.5),
        'ffn_w2': nrm(ks[12], (DEPTH, D_FF, D_MODEL), D_FF ** -0.5),
        'pc_w_in': nrm(ks[13], (N_EVEN, D_MODEL, POOL_WIDTH + 2 * CONV_WIDTH), D_MODEL ** -0.5),
        'pool_w': nrm(ks[14], (N_EVEN, N_POOL_GROUPS, POOL_GROUP, POOL_GROUP), POOL_GROUP ** -0.5),
        'pool_scale': 1.0 + nrm(ks[15], (N_EVEN, POOL_WIDTH), 0.1),
        'conv_dw': nrm(ks[16], (N_EVEN, CONV_K, CONV_WIDTH), CONV_K ** -0.5),
        'conv_ln_g': 1.0 + nrm(ks[17], (N_EVEN, CONV_WIDTH), 0.1),
        'conv_ln_b': nrm(ks[18], (N_EVEN, CONV_WIDTH), 0.02),
        'pc_w_out': nrm(ks[19], (N_EVEN, POOL_WIDTH + CONV_WIDTH, D_MODEL), (POOL_WIDTH + CONV_WIDTH) ** -0.5),
        'na_w_qkv': nrm(ks[20], (N_ODD, D_MODEL, 3 * D_MODEL), D_MODEL ** -0.5),
        'na_q_g': 1.0 + nrm(ks[21], (N_ODD, HEAD_DIM), 0.1),
        'na_k_g': 1.0 + nrm(ks[22], (N_ODD, HEAD_DIM), 0.1),
        'na_rpb': nrm(ks[23], (N_ODD, N_HEADS, 2 * WIN_ROWS_MAX - 1, 2 * WIN_COLS - 1), 0.1),
        'na_w_o': nrm(ks[24], (N_ODD, D_MODEL, D_MODEL), D_MODEL ** -0.5),
    }


def reference(x_prompt, x_sample, cache_k_na, cache_v_na, c, c_ctx, ada_w, ada_b, norm1_g, norm2_g,
              ffn_w1, ffn_w3, ffn_w2, pc_w_in, pool_w, pool_scale, conv_dw, conv_ln_g, conv_ln_b,
              pc_w_out, na_w_qkv, na_q_g, na_k_g, na_rpb, na_w_o):
    xp, xs = x_prompt, x_sample
    new_k, new_v = [], []
    for i in range(DEPTH):
        sh1p, sc1p, g1p, sh2p, sc2p, g2p = ada_mod(c_ctx[None], ada_w[i], ada_b[i])
        sh1s, sc1s, g1s, sh2s, sc2s, g2s = ada_mod(c, ada_w[i], ada_b[i])
        hp = rmsnorm(xp, norm1_g[i]) * (1.0 + sc1p) + sh1p
        hs = rmsnorm(xs, norm1_g[i]) * (1.0 + sc1s) + sh1s
        if i % 2 == 0:
            e = i // 2
            args = (pc_w_in[e], pool_w[e], pool_scale[e], conv_dw[e], conv_ln_g[e], conv_ln_b[e], pc_w_out[e])
            yp = even_mixer(hp, *args)
            ys = even_mixer(hs, *args)
        else:
            o = i // 2
            qp, kp, vp = qkv_heads(hp, na_w_qkv[o], na_q_g[o], na_k_g[o])
            yp = context_attention(qp, kp, vp) @ na_w_o[o]
            new_k.append(kp)
            new_v.append(vp)
            qs, ks_, vs = qkv_heads(hs, na_w_qkv[o], na_q_g[o], na_k_g[o])
            ys = neighborhood_attention(qs, ks_, vs, cache_k_na[:, o], cache_v_na[:, o], na_rpb[o]) @ na_w_o[o]
        xp = xp + g1p * yp
        xs = xs + g1s * ys
        hp = rmsnorm(xp, norm2_g[i]) * (1.0 + sc2p) + sh2p
        hs = rmsnorm(xs, norm2_g[i]) * (1.0 + sc2s) + sh2s
        xp = xp + g2p * swiglu(hp, ffn_w1[i], ffn_w3[i], ffn_w2[i])
        xs = xs + g2s * swiglu(hs, ffn_w1[i], ffn_w3[i], ffn_w2[i])
    k_state = jnp.stack(new_k, axis=1)
    v_state = jnp.stack(new_v, axis=1)
    return (xp, xs, k_state, v_state)
```

```python
import functools

import numpy as np
import jax
import jax.numpy as jnp
from jax import lax
from jax.experimental import pallas as pl
from jax.experimental.pallas import tpu as pltpu

F32 = jnp.float32
BF16 = jnp.bfloat16

D_MODEL = 4096
BATCH = 16
SEQ = 256
DEPTH = 2
DEC_BATCH = 8
DEC_SEQ = 1024
PAST_LEN = 512
GRID_W = 64
HEAD_DIM = 128
N_HEADS = D_MODEL // HEAD_DIM
POOL_WIDTH = D_MODEL // 2
CONV_WIDTH = D_MODEL // 2
POOL_WINDOWS = (2, 4, 8, 16)
POOL_GROUP = POOL_WIDTH // len(POOL_WINDOWS)
CONV_K = 31
WIN_ROWS = 8
WIN_COLS = 16
D_FF = 11008
RMS_EPS = 1e-6
LN_EPS = 1e-5

M_PROMPT = BATCH * SEQ
M_SAMPLE = DEC_BATCH * DEC_SEQ
M_TOTAL = M_PROMPT + M_SAMPLE
N_COND = 16
GRID_ROWS = DEC_SEQ // GRID_W
N_BIAS_PATTERNS = 8

V7X_VMEM_BYTES = 64 * 1024 * 1024
V7X_SUBLANES = 8
V7X_LANES = 128
COMPILER_SCRATCH_BYTES = 6 * 1024 * 1024

HALO = 16
SEQ_TILE = 256


def _params(vmem_bytes, n_axes):
    limit = min(int(vmem_bytes) + COMPILER_SCRATCH_BYTES, V7X_VMEM_BYTES - 2 * 1024 * 1024)
    return pltpu.CompilerParams(dimension_semantics=("arbitrary",) * n_axes, vmem_limit_bytes=limit)


def _cond_row(row_start):
    return jnp.maximum(row_start // DEC_SEQ - (M_PROMPT // DEC_SEQ - 1), 0)


def _ada_kernel(c_ref, w_ref, b_ref, o_ref):
    c = c_ref[...]
    s = (c * jax.nn.sigmoid(c)).astype(BF16)
    o_ref[...] = jnp.dot(s, w_ref[...].astype(BF16), preferred_element_type=F32) + b_ref[...]


def _ada_mod(cond, ada_w, ada_b):
    tn = 512
    n = 6 * D_MODEL
    vmem = 2 * D_MODEL * tn * 4 + D_MODEL * tn * 2 + 2 * N_COND * D_MODEL * 4
    return pl.pallas_call(
        _ada_kernel,
        out_shape=jax.ShapeDtypeStruct((DEPTH, N_COND, n), F32),
        grid=(DEPTH, n // tn),
        in_specs=[
            pl.BlockSpec((N_COND, D_MODEL), lambda l, j: (0, 0)),
            pl.BlockSpec((None, D_MODEL, tn), lambda l, j: (l, 0, j)),
            pl.BlockSpec((None, 1, tn), lambda l, j: (l, 0, j)),
        ],
        out_specs=pl.BlockSpec((None, N_COND, tn), lambda l, j: (l, 0, j)),
        compiler_params=_params(vmem, 2),
        name="ada_mod",
    )(cond, ada_w, ada_b.reshape(DEPTH, 1, n))


def _norm_mod_kernel(x_ref, g_ref, sc_ref, sh_ref, o_ref):
    x = x_ref[...]
    ms = jnp.mean(x * x, axis=-1, keepdims=True)
    y = x * lax.rsqrt(ms + RMS_EPS) * g_ref[...]
    o_ref[...] = (y * (1.0 + sc_ref[...]) + sh_ref[...]).astype(o_ref.dtype)


def _norm_mod(x, g, scale, shift):
    tm = 256
    mod_spec = pl.BlockSpec((None, 1, D_MODEL), lambda i: (_cond_row(i * tm), 0, 0))
    vmem = 2 * tm * D_MODEL * (4 + 2) + 4 * tm * D_MODEL * 4
    return pl.pallas_call(
        _norm_mod_kernel,
        out_shape=jax.ShapeDtypeStruct((M_TOTAL, D_MODEL), BF16),
        grid=(M_TOTAL // tm,),
        in_specs=[
            pl.BlockSpec((tm, D_MODEL), lambda i: (i, 0)),
            pl.BlockSpec((1, D_MODEL), lambda i: (0, 0)),
            mod_spec,
            mod_spec,
        ],
        out_specs=pl.BlockSpec((tm, D_MODEL), lambda i: (i, 0)),
        compiler_params=_params(vmem, 1),
        name="norm_mod",
    )(x, g.reshape(1, D_MODEL), scale, shift)


def _cast_weights_once(w_refs, wb_refs):
    @pl.when(pl.program_id(1) == 0)
    def _():
        for w_ref, wb_ref in zip(w_refs, wb_refs):
            wb_ref[...] = w_ref[...].astype(BF16)


def _mm_plain_kernel(x_ref, w_ref, o_ref, wb_ref):
    _cast_weights_once([w_ref], [wb_ref])
    o_ref[...] = jnp.dot(x_ref[...], wb_ref[...], preferred_element_type=F32).astype(o_ref.dtype)


def _mm_res_kernel(*refs, n_lhs, has_partial):
    x_refs = refs[:n_lhs]
    w_refs = refs[n_lhs:2 * n_lhs]
    rest = refs[2 * n_lhs:]
    if has_partial:
        p_ref, rest = rest[0], rest[1:]
    res_ref, gate_ref, o_ref = rest[:3]
    wb_refs = rest[3:]
    _cast_weights_once(w_refs, wb_refs)
    acc = jnp.dot(x_refs[0][...], wb_refs[0][...], preferred_element_type=F32)
    for x_ref, wb_ref in zip(x_refs[1:], wb_refs[1:]):
        acc = acc + jnp.dot(x_ref[...], wb_ref[...], preferred_element_type=F32)
    if has_partial:
        acc = acc + p_ref[...]
    o_ref[...] = res_ref[...] + gate_ref[...] * acc


def _mm_swiglu_kernel(x_ref, w1_ref, w3_ref, o_ref, wb1_ref, wb3_ref):
    _cast_weights_once([w1_ref, w3_ref], [wb1_ref, wb3_ref])
    x = x_ref[...]
    a = jnp.dot(x, wb1_ref[...], preferred_element_type=F32)
    b = jnp.dot(x, wb3_ref[...], preferred_element_type=F32)
    o_ref[...] = (a * jax.nn.sigmoid(a) * b).astype(o_ref.dtype)


def _mm_qkv_kernel(x_ref, w_ref, g_ref, o_ref, wb_ref, *, tn):
    _cast_weights_once([w_ref], [wb_ref])
    o_ref[...] = jnp.dot(x_ref[...], wb_ref[...], preferred_element_type=F32)

    @pl.when(pl.program_id(0) < 2 * D_MODEL // tn)
    def _():
        for h in range(tn // HEAD_DIM):
            cols = slice(h * HEAD_DIM, (h + 1) * HEAD_DIM)
            z = o_ref[:, cols]
            ms = jnp.mean(z * z, axis=-1, keepdims=True)
            o_ref[:, cols] = z * lax.rsqrt(ms + RMS_EPS) * g_ref[...]


def _mm_vmem(tm, tn, kb, n_lhs_bytes, n_w, out_bytes, extra_tiles):
    x = 2 * tm * n_lhs_bytes
    w = n_w * kb * tn * (2 * 4 + 2)
    o = 2 * tm * tn * out_bytes
    e = extra_tiles * 2 * tm * tn * 4
    return x + w + o + e + 2 * tm * tn * 4


def _mm_plain(x, w, *, tm, tn, kb, kblk, out_dtype):
    m = x.shape[0]
    n = w.shape[1]
    vmem = _mm_vmem(tm, tn, kb, kb * 2, 1, jnp.dtype(out_dtype).itemsize, 0)
    return pl.pallas_call(
        _mm_plain_kernel,
        out_shape=jax.ShapeDtypeStruct((m, n), out_dtype),
        grid=(n // tn, m // tm),
        in_specs=[
            pl.BlockSpec((tm, kb), lambda j, i: (i, kblk)),
            pl.BlockSpec((kb, tn), lambda j, i: (kblk, j)),
        ],
        out_specs=pl.BlockSpec((tm, tn), lambda j, i: (i, j)),
        scratch_shapes=[pltpu.VMEM((kb, tn), BF16)],
        compiler_params=_params(vmem, 2),
        name="mm_plain",
    )(x, w)


def _mm_res(xs, ws, xblks, wblks, kb, res, gate, partial=None, *, tm, tn):
    m, n = res.shape
    n_lhs = len(xs)
    tile = pl.BlockSpec((tm, tn), lambda j, i: (i, j))
    in_specs = [pl.BlockSpec((tm, kb), functools.partial(lambda j, i, kk: (i, kk), kk=kk)) for kk in xblks]
    in_specs += [pl.BlockSpec((kb, tn), functools.partial(lambda j, i, kk: (kk, j), kk=kk)) for kk in wblks]
    args = list(xs) + list(ws)
    if partial is not None:
        in_specs.append(tile)
        args.append(partial)
    in_specs += [tile, pl.BlockSpec((None, 1, tn), lambda j, i: (_cond_row(i * tm), 0, j))]
    args += [res, gate]
    vmem = _mm_vmem(tm, tn, kb, n_lhs * kb * 2, n_lhs, 4, 1 + (partial is not None))
    return pl.pallas_call(
        functools.partial(_mm_res_kernel, n_lhs=n_lhs, has_partial=partial is not None),
        out_shape=jax.ShapeDtypeStruct((m, n), F32),
        grid=(n // tn, m // tm),
        in_specs=in_specs,
        out_specs=tile,
        scratch_shapes=[pltpu.VMEM((kb, tn), BF16) for _ in range(n_lhs)],
        compiler_params=_params(vmem, 2),
        name="mm_res",
    )(*args)


def _mm_swiglu(x, w1, w3, *, tm, tn):
    m, k = x.shape
    n = w1.shape[1]
    w_spec = pl.BlockSpec((k, tn), lambda j, i: (0, j))
    vmem = _mm_vmem(tm, tn, k, k * 2, 2, 2, 2)
    return pl.pallas_call(
        _mm_swiglu_kernel,
        out_shape=jax.ShapeDtypeStruct((m, n), BF16),
        grid=(n // tn, m // tm),
        in_specs=[pl.BlockSpec((tm, k), lambda j, i: (i, 0)), w_spec, w_spec],
        out_specs=pl.BlockSpec((tm, tn), lambda j, i: (i, j)),
        scratch_shapes=[pltpu.VMEM((k, tn), BF16), pltpu.VMEM((k, tn), BF16)],
        compiler_params=_params(vmem, 2),
        name="mm_swiglu",
    )(x, w1, w3)


def _mm_qkv(x, w, gains, *, tm, tn):
    m, k = x.shape
    n = w.shape[1]
    tiles_per_section = D_MODEL // tn
    vmem = _mm_vmem(tm, tn, k, k * 2, 1, 4, 1)
    return pl.pallas_call(
        functools.partial(_mm_qkv_kernel, tn=tn),
        out_shape=jax.ShapeDtypeStruct((m, n), F32),
        grid=(n // tn, m // tm),
        in_specs=[
            pl.BlockSpec((tm, k), lambda j, i: (i, 0)),
            pl.BlockSpec((k, tn), lambda j, i: (0, j)),
            pl.BlockSpec((None, 1, HEAD_DIM), lambda j, i: (jnp.minimum(j // tiles_per_section, 1), 0, 0)),
        ],
        out_specs=pl.BlockSpec((tm, tn), lambda j, i: (i, j)),
        scratch_shapes=[pltpu.VMEM((k, tn), BF16)],
        compiler_params=_params(vmem, 2),
        name="mm_qkv",
    )(x, w, gains)


def _seq_tile_position(i):
    is_prompt = i < M_PROMPT // SEQ_TILE
    tiles_per_seq = DEC_SEQ // SEQ_TILE
    off = jnp.where(is_prompt, 0, (jnp.maximum(i - M_PROMPT // SEQ_TILE, 0) % tiles_per_seq) * SEQ_TILE)
    length = jnp.where(is_prompt, SEQ, DEC_SEQ)
    return off, length


def _halo_specs(width, col_block):
    n_halo_blocks = M_TOTAL // HALO
    per_tile = SEQ_TILE // HALO
    main = pl.BlockSpec((SEQ_TILE, width), lambda i: (i, col_block))
    prev = pl.BlockSpec((HALO, width), lambda i: (jnp.maximum(i * per_tile - 1, 0), col_block))
    nxt = pl.BlockSpec((HALO, width), lambda i: (jnp.minimum((i + 1) * per_tile, n_halo_blocks - 1), col_block))
    return main, prev, nxt


def _pool_kernel(zm_ref, zp_ref, zn_ref, w_ref, s_ref, o_ref, buf_ref, wb_ref):
    i = pl.program_id(0)

    @pl.when(i == 0)
    def _():
        wb_ref[...] = w_ref[...].astype(BF16)

    off, length = _seq_tile_position(i)
    buf_ref[0:HALO, :] = jnp.where(off == 0, 0.0, zp_ref[...])
    buf_ref[HALO:HALO + SEQ_TILE, :] = zm_ref[...]
    buf_ref[HALO + SEQ_TILE:, :] = jnp.where(off + SEQ_TILE == length, 0.0, zn_ref[...])

    rows = 64
    for g, win in enumerate(POOL_WINDOWS):
        lo, hi = win // 2, win - win // 2 - 1
        cols = slice(g * POOL_GROUP, (g + 1) * POOL_GROUP)
        w_g = wb_ref[g]
        for rc in range(SEQ_TILE // rows):
            r0 = rc * rows
            acc = buf_ref[HALO + r0 - lo:HALO + r0 - lo + rows, cols]
            for s in range(-lo + 1, hi + 1):
                acc = acc + buf_ref[HALO + r0 + s:HALO + r0 + s + rows, cols]
            t = off + r0 + lax.broadcasted_iota(jnp.int32, (rows, 1), 0)
            count = (jnp.minimum(t + hi + 1, length) - jnp.maximum(t - lo, 0)).astype(F32)
            mixed = (acc / count - zm_ref[r0:r0 + rows, cols]).astype(BF16)
            y = jnp.dot(mixed, w_g, preferred_element_type=F32) * s_ref[:, cols]
            o_ref[r0:r0 + rows, cols] = y.astype(o_ref.dtype)


def _pool_mix(z, pool_w, pool_scale):
    main, prev, nxt = _halo_specs(POOL_WIDTH, 0)
    n_g = len(POOL_WINDOWS)
    vmem = (2 * (SEQ_TILE + 2 * HALO) * POOL_WIDTH * 4 + (SEQ_TILE + 2 * HALO) * POOL_WIDTH * 4
            + n_g * POOL_GROUP * POOL_GROUP * (2 * 4 + 2) + 2 * SEQ_TILE * POOL_WIDTH * 2)
    return pl.pallas_call(
        _pool_kernel,
        out_shape=jax.ShapeDtypeStruct((M_TOTAL, POOL_WIDTH), BF16),
        grid=(M_TOTAL // SEQ_TILE,),
        in_specs=[
            main, prev, nxt,
            pl.BlockSpec((n_g, POOL_GROUP, POOL_GROUP), lambda i: (0, 0, 0)),
            pl.BlockSpec((1, POOL_WIDTH), lambda i: (0, 0)),
        ],
        out_specs=pl.BlockSpec((SEQ_TILE, POOL_WIDTH), lambda i: (i, 0)),
        scratch_shapes=[
            pltpu.VMEM((SEQ_TILE + 2 * HALO, POOL_WIDTH), F32),
            pltpu.VMEM((n_g, POOL_GROUP, POOL_GROUP), BF16),
        ],
        compiler_params=_params(vmem, 1),
        name="pool_mix",
    )(z, z, z, pool_w, pool_scale.reshape(1, POOL_WIDTH))


def _conv_kernel(am_ref, ap_ref, an_ref, gm_ref, gp_ref, gn_ref, w_ref, lg_ref, lb_ref, o_ref, buf_ref, y_ref):
    i = pl.program_id(0)
    off, length = _seq_tile_position(i)

    def glu(a_ref, g_ref):
        return a_ref[...] * jax.nn.sigmoid(g_ref[...])

    buf_ref[0:HALO, :] = jnp.where(off == 0, 0.0, glu(ap_ref, gp_ref))
    buf_ref[HALO:HALO + SEQ_TILE, :] = glu(am_ref, gm_ref)
    buf_ref[HALO + SEQ_TILE:, :] = jnp.where(off + SEQ_TILE == length, 0.0, glu(an_ref, gn_ref))

    rows = 128
    first_tap_row = HALO - CONV_K // 2

    def channel_tile(c, carry):
        cols = pl.ds(pl.multiple_of(c * V7X_LANES, V7X_LANES), V7X_LANES)
        taps = [w_ref[k:k + 1, cols] for k in range(CONV_K)]
        for rc in range(SEQ_TILE // rows):
            r0 = first_tap_row + rc * rows
            acc = buf_ref[r0:r0 + rows, cols] * taps[0]
            for k in range(1, CONV_K):
                acc = acc + buf_ref[r0 + k:r0 + k + rows, cols] * taps[k]
            y_ref[rc * rows:(rc + 1) * rows, cols] = acc
        return carry

    lax.fori_loop(0, CONV_WIDTH // V7X_LANES, channel_tile, 0)

    ln_rows = 64
    for rc in range(SEQ_TILE // ln_rows):
        rs = slice(rc * ln_rows, (rc + 1) * ln_rows)
        y = y_ref[rs, :]
        mu = jnp.mean(y, axis=-1, keepdims=True)
        d = y - mu
        var = jnp.mean(d * d, axis=-1, keepdims=True)
        yn = d * lax.rsqrt(var + LN_EPS) * lg_ref[...] + lb_ref[...]
        o_ref[rs, :] = (yn * jax.nn.sigmoid(yn)).astype(o_ref.dtype)


def _conv_module(z, conv_dw, ln_g, ln_b):
    am, ap, an = _halo_specs(CONV_WIDTH, 1)
    gm, gp, gn = _halo_specs(CONV_WIDTH, 2)
    vec = pl.BlockSpec((1, CONV_WIDTH), lambda i: (0, 0))
    tile_rows = SEQ_TILE + 2 * HALO
    vmem = (2 * 2 * tile_rows * CONV_WIDTH * 4 + tile_rows * CONV_WIDTH * 4 + SEQ_TILE * CONV_WIDTH * 4
            + 2 * SEQ_TILE * CONV_WIDTH * 2 + 4 * SEQ_TILE * CONV_WIDTH * 4)
    return pl.pallas_call(
        _conv_kernel,
        out_shape=jax.ShapeDtypeStruct((M_TOTAL, CONV_WIDTH), BF16),
        grid=(M_TOTAL // SEQ_TILE,),
        in_specs=[am, ap, an, gm, gp, gn, pl.BlockSpec((CONV_K, CONV_WIDTH), lambda i: (0, 0)), vec, vec],
        out_specs=pl.BlockSpec((SEQ_TILE, CONV_WIDTH), lambda i: (i, 0)),
        scratch_shapes=[
            pltpu.VMEM((tile_rows, CONV_WIDTH), F32),
            pltpu.VMEM((SEQ_TILE, CONV_WIDTH), F32),
        ],
        compiler_params=_params(vmem, 1),
        name="conv_module",
    )(z, z, z, z, z, z, conv_dw, ln_g.reshape(1, CONV_WIDTH), ln_b.reshape(1, CONV_WIDTH))


_NT_DIMS = (((1,), (1,)), ((), ()))
ATTN_SCALE = HEAD_DIM ** -0.5


def _ctx_attn_kernel(q_ref, k_ref, v_ref, o_ref, *, heads):
    for h in range(heads):
        cols = slice(h * HEAD_DIM, (h + 1) * HEAD_DIM)
        q = q_ref[:, cols].astype(BF16)
        k = k_ref[:, cols].astype(BF16)
        v = v_ref[:, cols].astype(BF16)
        s = lax.dot_general(q, k, _NT_DIMS, preferred_element_type=F32) * ATTN_SCALE
        e = jnp.exp(s - jnp.max(s, axis=-1, keepdims=True))
        l = jnp.sum(e, axis=-1, keepdims=True)
        o = jnp.dot(e.astype(BF16), v, preferred_element_type=F32) / l
        o_ref[:, cols] = o.astype(o_ref.dtype)


def _ctx_attention(qkv):
    heads = 4
    width = heads * HEAD_DIM
    sections = D_MODEL // width

    def spec(section):
        return pl.BlockSpec((SEQ, width), lambda b, g: (b, section * sections + g))

    vmem = 2 * 3 * SEQ * width * 4 + 2 * SEQ * width * 2 + 8 * SEQ * SEQ * 4
    return pl.pallas_call(
        functools.partial(_ctx_attn_kernel, heads=heads),
        out_shape=jax.ShapeDtypeStruct((M_TOTAL, D_MODEL), BF16),
        grid=(BATCH, sections),
        in_specs=[spec(0), spec(1), spec(2)],
        out_specs=pl.BlockSpec((SEQ, width), lambda b, g: (b, g)),
        compiler_params=_params(vmem, 2),
        name="ctx_attention",
    )(qkv, qkv, qkv)


def _window_start(r):
    return min(max(r - WIN_ROWS // 2, 0), GRID_ROWS - WIN_ROWS)


def _bias_pattern(r):
    lo, hi = WIN_ROWS // 2, GRID_ROWS - WIN_ROWS // 2
    if r < lo:
        return r
    if r <= hi:
        return lo
    return r - (hi - lo)


def _na_attn_kernel(q_ref, k_ref, v_ref, kc_ref, vc_ref, b_ref, prev_ref, o_ref):
    del prev_ref
    q = q_ref[...].astype(BF16)
    k = k_ref[...].astype(BF16)
    v = v_ref[...].astype(BF16)
    kc = kc_ref[...].astype(BF16)
    vc = vc_ref[...].astype(BF16)

    s_c = lax.dot_general(q, kc, _NT_DIMS, preferred_element_type=F32) * ATTN_SCALE
    m_c = jnp.max(s_c, axis=-1, keepdims=True)
    e_c = jnp.exp(s_c - m_c)
    l_c = jnp.sum(e_c, axis=-1, keepdims=True)
    o_c = jnp.dot(e_c.astype(BF16), vc, preferred_element_type=F32)

    n_keys = WIN_ROWS * GRID_W
    for r in range(GRID_ROWS):
        rows = slice(r * GRID_W, (r + 1) * GRID_W)
        keys = slice(_window_start(r) * GRID_W, _window_start(r) * GRID_W + n_keys)
        s = lax.dot_general(q[rows], k[keys], _NT_DIMS, preferred_element_type=F32) * ATTN_SCALE
        s = s + b_ref[_bias_pattern(r)]
        m = jnp.maximum(jnp.max(s, axis=-1, keepdims=True), m_c[rows])
        e = jnp.exp(s - m)
        a = jnp.exp(m_c[rows] - m)
        l = jnp.sum(e, axis=-1, keepdims=True) + a * l_c[rows]
        o = jnp.dot(e.astype(BF16), v[keys], preferred_element_type=F32) + a * o_c[rows]
        o_ref[rows, :] = (o / l).astype(o_ref.dtype)


def _na_attention(qkv, cache_k, cache_v, bias, attn_out):
    row0 = M_PROMPT // DEC_SEQ

    def spec(section):
        return pl.BlockSpec((DEC_SEQ, HEAD_DIM), lambda b, h: (row0 + b, section * N_HEADS + h))

    cache_spec = pl.BlockSpec((None, PAST_LEN, HEAD_DIM), lambda b, h: (b, 0, h))
    n_keys = WIN_ROWS * GRID_W
    vmem = (2 * 3 * DEC_SEQ * HEAD_DIM * 4 + 2 * 2 * PAST_LEN * HEAD_DIM * 4
            + 2 * N_BIAS_PATTERNS * GRID_W * n_keys * 4 + 2 * DEC_SEQ * HEAD_DIM * 2
            + 6 * DEC_SEQ * PAST_LEN * 4)
    return pl.pallas_call(
        _na_attn_kernel,
        out_shape=jax.ShapeDtypeStruct((M_TOTAL, D_MODEL), BF16),
        grid=(DEC_BATCH, N_HEADS),
        in_specs=[
            spec(0), spec(1), spec(2), cache_spec, cache_spec,
            pl.BlockSpec((None, N_BIAS_PATTERNS, GRID_W, n_keys), lambda b, h: (h, 0, 0, 0)),
            pl.BlockSpec(memory_space=pl.ANY),
        ],
        out_specs=pl.BlockSpec((DEC_SEQ, HEAD_DIM), lambda b, h: (row0 + b, h)),
        input_output_aliases={6: 0},
        compiler_params=_params(vmem, 2),
        name="na_attention",
    )(qkv, qkv, qkv, cache_k, cache_v, bias, attn_out)


def _na_bias_table(rpb):
    rep_rows = [r for r in range(GRID_ROWS) if r == 0 or _bias_pattern(r) != _bias_pattern(r - 1)]
    dr = np.array([[_window_start(r) + i - r + WIN_ROWS - 1 for i in range(WIN_ROWS)] for r in rep_rows])
    qc = np.arange(GRID_W)
    kc = np.arange(GRID_W)
    cstart = np.clip(qc - WIN_COLS // 2, 0, GRID_W - WIN_COLS)
    col_ok = (kc[None, :] >= cstart[:, None]) & (kc[None, :] < cstart[:, None] + WIN_COLS)
    dc = np.clip(kc[None, :] - qc[:, None], -(WIN_COLS - 1), WIN_COLS - 1) + WIN_COLS - 1
    table = rpb[:, dr[:, None, :, None], dc[None, :, None, :]]
    table = jnp.where(col_ok[None, None, :, None, :], table, -jnp.inf)
    return table.reshape(N_HEADS, N_BIAS_PATTERNS, GRID_W, WIN_ROWS * GRID_W).astype(F32)


def kernel(x_prompt, x_sample, cache_k_na, cache_v_na, c, c_ctx, ada_w, ada_b, norm1_g, norm2_g, ffn_w1, ffn_w3,
           ffn_w2, pc_w_in, pool_w, pool_scale, conv_dw, conv_ln_g, conv_ln_b, pc_w_out, na_w_qkv, na_q_g, na_k_g,
           na_rpb, na_w_o):
    x = jnp.concatenate([x_prompt.reshape(M_PROMPT, D_MODEL), x_sample.reshape(M_SAMPLE, D_MODEL)], axis=0)
    cond = jnp.concatenate([c_ctx[None], c, jnp.zeros((N_COND - 1 - DEC_BATCH, D_MODEL), F32)], axis=0)
    mod = _ada_mod(cond, ada_w, ada_b)
    mod = mod.reshape(DEPTH, N_COND, 6, D_MODEL).transpose(0, 2, 1, 3).reshape(DEPTH, 6, N_COND, 1, D_MODEL)

    def ffn(x, layer):
        h = _norm_mod(x, norm2_g[layer], mod[layer, 4], mod[layer, 3])
        hidden = _mm_swiglu(h, ffn_w1[layer], ffn_w3[layer], tm=1024, tn=256)
        half = D_FF // 2
        part = _mm_plain(hidden, ffn_w2[layer], tm=512, tn=512, kb=half, kblk=0, out_dtype=F32)
        return _mm_res([hidden], [ffn_w2[layer]], [1], [1], half, x, mod[layer, 5], part, tm=512, tn=512)

    h = _norm_mod(x, norm1_g[0], mod[0, 1], mod[0, 0])
    z = _mm_plain(h, pc_w_in[0], tm=1024, tn=512, kb=D_MODEL, kblk=0, out_dtype=F32)
    y_pool = _pool_mix(z, pool_w[0], pool_scale[0])
    y_conv = _conv_module(z, conv_dw[0], conv_ln_g[0], conv_ln_b[0])
    x = _mm_res([y_pool, y_conv], [pc_w_out[0], pc_w_out[0]], [0, 0], [0, 1], POOL_WIDTH, x, mod[0, 2],
                tm=1024, tn=512)
    x = ffn(x, 0)

    h = _norm_mod(x, norm1_g[1], mod[1, 1], mod[1, 0])
    gains = jnp.stack([na_q_g[0], na_k_g[0]]).reshape(2, 1, HEAD_DIM)
    qkv = _mm_qkv(h, na_w_qkv[0], gains, tm=1024, tn=512)
    new_k = qkv[:M_PROMPT, D_MODEL:2 * D_MODEL].reshape(BATCH, 1, SEQ, N_HEADS, HEAD_DIM)
    new_v = qkv[:M_PROMPT, 2 * D_MODEL:].reshape(BATCH, 1, SEQ, N_HEADS, HEAD_DIM)
    attn = _ctx_attention(qkv)
    attn = _na_attention(qkv, cache_k_na.reshape(DEC_BATCH, PAST_LEN, D_MODEL),
                         cache_v_na.reshape(DEC_BATCH, PAST_LEN, D_MODEL), _na_bias_table(na_rpb[0]), attn)
    x = _mm_res([attn], [na_w_o[0]], [0], [0], D_MODEL, x, mod[1, 2], tm=1024, tn=512)
    x = ffn(x, 1)

    y_prompt = x[:M_PROMPT].reshape(BATCH, SEQ, D_MODEL)
    y_sample = x[M_PROMPT:].reshape(DEC_BATCH, DEC_SEQ, D_MODEL)
    return (y_prompt, y_sample, new_k, new_v)
```

```python
import functools

import numpy as np
import jax
import jax.numpy as jnp
from jax import lax
from jax.experimental import pallas as pl
from jax.experimental.pallas import tpu as pltpu

F32 = jnp.float32
BF16 = jnp.bfloat16

D_MODEL = 4096
BATCH = 16
SEQ = 256
DEPTH = 2
DEC_BATCH = 8
DEC_SEQ = 1024
PAST_LEN = 512
GRID_W = 64
HEAD_DIM = 128
N_HEADS = D_MODEL // HEAD_DIM
POOL_WIDTH = D_MODEL // 2
CONV_WIDTH = D_MODEL // 2
POOL_WINDOWS = (2, 4, 8, 16)
POOL_GROUP = POOL_WIDTH // len(POOL_WINDOWS)
CONV_K = 31
WIN_ROWS = 8
WIN_COLS = 16
D_FF = 11008
RMS_EPS = 1e-6
LN_EPS = 1e-5

M_PROMPT = BATCH * SEQ
M_SAMPLE = DEC_BATCH * DEC_SEQ
M_TOTAL = M_PROMPT + M_SAMPLE
N_COND = 16
GRID_ROWS = DEC_SEQ // GRID_W
N_LOCAL_KEYS = WIN_ROWS * GRID_W
N_REL_ROWS = 2 * WIN_ROWS - 1

V7X_VMEM_BYTES = 64 * 1024 * 1024
V7X_LANES = 128
COMPILER_SCRATCH_BYTES = 6 * 1024 * 1024

HALO = 16
SEQ_TILE = 256


def _params(vmem_bytes, n_axes):
    limit = min(int(vmem_bytes) + COMPILER_SCRATCH_BYTES, V7X_VMEM_BYTES - 2 * 1024 * 1024)
    return pltpu.CompilerParams(dimension_semantics=("arbitrary",) * n_axes, vmem_limit_bytes=limit)


def _cond_row(row_start):
    return jnp.maximum(row_start // DEC_SEQ - (M_PROMPT // DEC_SEQ - 1), 0)


def _split_specs(tm, tn, n_grid_axes):
    n_p = M_PROMPT // tm
    if n_grid_axes == 1:
        return (pl.BlockSpec((tm, tn), lambda i: (jnp.minimum(i, n_p - 1), 0)),
                pl.BlockSpec((tm, tn), lambda i: (jnp.maximum(i - n_p, 0), 0)))
    return (pl.BlockSpec((tm, tn), lambda j, i: (jnp.minimum(i, n_p - 1), j)),
            pl.BlockSpec((tm, tn), lambda j, i: (jnp.maximum(i - n_p, 0), j)))


def _ada_kernel(c_ref, w_ref, b_ref, o_ref):
    c = c_ref[...]
    s = (c * jax.nn.sigmoid(c)).astype(BF16)
    o_ref[...] = jnp.dot(s, w_ref[...].astype(BF16), preferred_element_type=F32) + b_ref[...]


def _ada_mod(cond, ada_w, ada_b):
    tn = 512
    n = 6 * D_MODEL
    vmem = 2 * D_MODEL * tn * 4 + D_MODEL * tn * 2 + 2 * N_COND * D_MODEL * 4
    return pl.pallas_call(
        _ada_kernel,
        out_shape=jax.ShapeDtypeStruct((DEPTH, N_COND, n), F32),
        grid=(DEPTH, n // tn),
        in_specs=[
            pl.BlockSpec((N_COND, D_MODEL), lambda l, j: (0, 0)),
            pl.BlockSpec((None, D_MODEL, tn), lambda l, j: (l, 0, j)),
            pl.BlockSpec((None, 1, tn), lambda l, j: (l, 0, j)),
        ],
        out_specs=pl.BlockSpec((None, N_COND, tn), lambda l, j: (l, 0, j)),
        compiler_params=_params(vmem, 2),
        name="ada_mod",
    )(cond, ada_w, ada_b.reshape(DEPTH, 1, n))


def _norm_mod_math(x, g_ref, sc_ref, sh_ref):
    ms = jnp.mean(x * x, axis=-1, keepdims=True)
    y = x * lax.rsqrt(ms + RMS_EPS) * g_ref[...]
    return y * (1.0 + sc_ref[...]) + sh_ref[...]


def _norm_mod_kernel(x_ref, g_ref, sc_ref, sh_ref, o_ref):
    o_ref[...] = _norm_mod_math(x_ref[...], g_ref, sc_ref, sh_ref).astype(o_ref.dtype)


def _norm_mod_split_kernel(xp_ref, xs_ref, g_ref, sc_ref, sh_ref, o_ref, *, n_p):
    @pl.when(pl.program_id(0) < n_p)
    def _():
        o_ref[...] = _norm_mod_math(xp_ref[...], g_ref, sc_ref, sh_ref).astype(o_ref.dtype)

    @pl.when(pl.program_id(0) >= n_p)
    def _():
        o_ref[...] = _norm_mod_math(xs_ref[...], g_ref, sc_ref, sh_ref).astype(o_ref.dtype)


def _norm_mod(x, g, scale, shift):
    tm = 256
    mod_spec = pl.BlockSpec((None, 1, D_MODEL), lambda i: (_cond_row(i * tm), 0, 0))
    if isinstance(x, tuple):
        body = functools.partial(_norm_mod_split_kernel, n_p=M_PROMPT // tm)
        x_specs = list(_split_specs(tm, D_MODEL, 1))
        xs = list(x)
    else:
        body = _norm_mod_kernel
        x_specs = [pl.BlockSpec((tm, D_MODEL), lambda i: (i, 0))]
        xs = [x]
    vmem = 2 * tm * D_MODEL * (4 * len(xs) + 2) + 4 * tm * D_MODEL * 4
    return pl.pallas_call(
        body,
        out_shape=jax.ShapeDtypeStruct((M_TOTAL, D_MODEL), BF16),
        grid=(M_TOTAL // tm,),
        in_specs=x_specs + [pl.BlockSpec((1, D_MODEL), lambda i: (0, 0)), mod_spec, mod_spec],
        out_specs=pl.BlockSpec((tm, D_MODEL), lambda i: (i, 0)),
        compiler_params=_params(vmem, 1),
        name="norm_mod",
    )(*xs, g.reshape(1, D_MODEL), scale, shift)


def _cast_weights_once(w_refs, wb_refs):
    @pl.when(pl.program_id(1) == 0)
    def _():
        for w_ref, wb_ref in zip(w_refs, wb_refs):
            wb_ref[...] = w_ref[...].astype(BF16)


def _w_spec(kb, tn, layer, kblk):
    return pl.BlockSpec((None, kb, tn), lambda j, i: (layer, kblk, j))


def _mm_plain_kernel(x_ref, w_ref, o_ref, wb_ref):
    _cast_weights_once([w_ref], [wb_ref])
    o_ref[...] = jnp.dot(x_ref[...], wb_ref[...], preferred_element_type=F32).astype(o_ref.dtype)


def _mm_res_kernel(*refs, n_lhs, has_partial, n_res, n_out, n_p):
    refs = list(refs)
    x_refs = [refs.pop(0) for _ in range(n_lhs)]
    w_refs = [refs.pop(0) for _ in range(n_lhs)]
    p_ref = refs.pop(0) if has_partial else None
    res_refs = [refs.pop(0) for _ in range(n_res)]
    gate_ref = refs.pop(0)
    out_refs = [refs.pop(0) for _ in range(n_out)]
    wb_refs = refs
    _cast_weights_once(w_refs, wb_refs)
    acc = jnp.dot(x_refs[0][...], wb_refs[0][...], preferred_element_type=F32)
    for x_ref, wb_ref in zip(x_refs[1:], wb_refs[1:]):
        acc = acc + jnp.dot(x_ref[...], wb_ref[...], preferred_element_type=F32)
    if has_partial:
        acc = acc + p_ref[...]
    is_prompt = pl.program_id(1) < n_p
    if n_res == 2:
        res = jnp.where(is_prompt, res_refs[0][...], res_refs[1][...])
    else:
        res = res_refs[0][...]
    y = res + gate_ref[...] * acc
    if n_out == 2:
        @pl.when(is_prompt)
        def _():
            out_refs[0][...] = y

        @pl.when(jnp.logical_not(is_prompt))
        def _():
            out_refs[1][...] = y
    else:
        out_refs[0][...] = y


def _mm_swiglu_kernel(x_ref, w1_ref, w3_ref, o_ref, wb1_ref, wb3_ref):
    _cast_weights_once([w1_ref, w3_ref], [wb1_ref, wb3_ref])
    x = x_ref[...]
    a = jnp.dot(x, wb1_ref[...], preferred_element_type=F32)
    b = jnp.dot(x, wb3_ref[...], preferred_element_type=F32)
    o_ref[...] = (a * jax.nn.sigmoid(a) * b).astype(o_ref.dtype)


def _mm_qkv_kernel(x_ref, w_ref, g_ref, o_ref, wb_ref, *, tn):
    _cast_weights_once([w_ref], [wb_ref])
    o_ref[...] = jnp.dot(x_ref[...], wb_ref[...], preferred_element_type=F32)

    @pl.when(pl.program_id(0) < 2 * D_MODEL // tn)
    def _():
        for h in range(tn // HEAD_DIM):
            cols = slice(h * HEAD_DIM, (h + 1) * HEAD_DIM)
            z = o_ref[:, cols]
            ms = jnp.mean(z * z, axis=-1, keepdims=True)
            o_ref[:, cols] = z * lax.rsqrt(ms + RMS_EPS) * g_ref[...]


def _mm_vmem(tm, tn, kb, n_lhs_bytes, n_w, out_bytes, extra_tiles):
    x = 2 * tm * n_lhs_bytes
    w = n_w * kb * tn * (2 * 4 + 2)
    o = 2 * tm * tn * out_bytes
    e = extra_tiles * 2 * tm * tn * 4
    return x + w + o + e + 2 * tm * tn * 4


def _mm_plain(x, w, layer, *, tm, tn, kb, kblk, out_dtype):
    m = x.shape[0]
    n = w.shape[2]
    vmem = _mm_vmem(tm, tn, kb, kb * 2, 1, jnp.dtype(out_dtype).itemsize, 0)
    return pl.pallas_call(
        _mm_plain_kernel,
        out_shape=jax.ShapeDtypeStruct((m, n), out_dtype),
        grid=(n // tn, m // tm),
        in_specs=[pl.BlockSpec((tm, kb), lambda j, i: (i, kblk)), _w_spec(kb, tn, layer, kblk)],
        out_specs=pl.BlockSpec((tm, tn), lambda j, i: (i, j)),
        scratch_shapes=[pltpu.VMEM((kb, tn), BF16)],
        compiler_params=_params(vmem, 2),
        name="mm_plain",
    )(x, w)


def _mm_res(xs, ws, layer, xblks, wblks, kb, res, gate, partial=None, *, tm, tn, split_out=False):
    n = ws[0].shape[2]
    n_lhs = len(xs)
    tile = pl.BlockSpec((tm, tn), lambda j, i: (i, j))
    in_specs = [pl.BlockSpec((tm, kb), functools.partial(lambda j, i, kk: (i, kk), kk=kk)) for kk in xblks]
    in_specs += [_w_spec(kb, tn, layer, kk) for kk in wblks]
    args = list(xs) + list(ws)
    if partial is not None:
        in_specs.append(tile)
        args.append(partial)
    res_list = list(res) if isinstance(res, tuple) else [res]
    in_specs += list(_split_specs(tm, tn, 2)) if len(res_list) == 2 else [tile]
    in_specs.append(pl.BlockSpec((None, 1, tn), lambda j, i: (_cond_row(i * tm), 0, j)))
    args += res_list + [gate]
    if split_out:
        out_shape = (jax.ShapeDtypeStruct((M_PROMPT, n), F32), jax.ShapeDtypeStruct((M_SAMPLE, n), F32))
        out_specs = _split_specs(tm, tn, 2)
    else:
        out_shape = jax.ShapeDtypeStruct((M_TOTAL, n), F32)
        out_specs = tile
    n_out = 2 if split_out else 1
    vmem = _mm_vmem(tm, tn, kb, n_lhs * kb * 2, n_lhs, 4 * n_out, len(res_list) + (partial is not None))
    return pl.pallas_call(
        functools.partial(_mm_res_kernel, n_lhs=n_lhs, has_partial=partial is not None, n_res=len(res_list),
                          n_out=n_out, n_p=M_PROMPT // tm),
        out_shape=out_shape,
        grid=(n // tn, M_TOTAL // tm),
        in_specs=in_specs,
        out_specs=out_specs,
        scratch_shapes=[pltpu.VMEM((kb, tn), BF16) for _ in range(n_lhs)],
        compiler_params=_params(vmem, 2),
        name="mm_res",
    )(*args)


def _mm_swiglu(x, w1, w3, layer, *, tm, tn):
    m, k = x.shape
    n = w1.shape[2]
    vmem = _mm_vmem(tm, tn, k, k * 2, 2, 2, 2)
    return pl.pallas_call(
        _mm_swiglu_kernel,
        out_shape=jax.ShapeDtypeStruct((m, n), BF16),
        grid=(n // tn, m // tm),
        in_specs=[pl.BlockSpec((tm, k), lambda j, i: (i, 0)), _w_spec(k, tn, layer, 0), _w_spec(k, tn, layer, 0)],
        out_specs=pl.BlockSpec((tm, tn), lambda j, i: (i, j)),
        scratch_shapes=[pltpu.VMEM((k, tn), BF16), pltpu.VMEM((k, tn), BF16)],
        compiler_params=_params(vmem, 2),
        name="mm_swiglu",
    )(x, w1, w3)


def _mm_qkv(x, w, gains, *, tm, tn):
    m, k = x.shape
    n = w.shape[2]
    tiles_per_section = D_MODEL // tn
    vmem = _mm_vmem(tm, tn, k, k * 2, 1, 4, 1)
    return pl.pallas_call(
        functools.partial(_mm_qkv_kernel, tn=tn),
        out_shape=jax.ShapeDtypeStruct((m, n), F32),
        grid=(n // tn, m // tm),
        in_specs=[
            pl.BlockSpec((tm, k), lambda j, i: (i, 0)),
            _w_spec(k, tn, 0, 0),
            pl.BlockSpec((None, 1, HEAD_DIM), lambda j, i: (jnp.minimum(j // tiles_per_section, 1), 0, 0)),
        ],
        out_specs=pl.BlockSpec((tm, tn), lambda j, i: (i, j)),
        scratch_shapes=[pltpu.VMEM((k, tn), BF16)],
        compiler_params=_params(vmem, 2),
        name="mm_qkv",
    )(x, w, gains)


def _seq_tile_position(i):
    is_prompt = i < M_PROMPT // SEQ_TILE
    tiles_per_seq = DEC_SEQ // SEQ_TILE
    off = jnp.where(is_prompt, 0, (jnp.maximum(i - M_PROMPT // SEQ_TILE, 0) % tiles_per_seq) * SEQ_TILE)
    length = jnp.where(is_prompt, SEQ, DEC_SEQ)
    return off, length


def _halo_specs(width, col_block):
    n_halo_blocks = M_TOTAL // HALO
    per_tile = SEQ_TILE // HALO
    main = pl.BlockSpec((SEQ_TILE, width), lambda i: (i, col_block))
    prev = pl.BlockSpec((HALO, width), lambda i: (jnp.maximum(i * per_tile - 1, 0), col_block))
    nxt = pl.BlockSpec((HALO, width), lambda i: (jnp.minimum((i + 1) * per_tile, n_halo_blocks - 1), col_block))
    return main, prev, nxt


def _pool_kernel(zm_ref, zp_ref, zn_ref, w_ref, s_ref, o_ref, buf_ref, wb_ref):
    i = pl.program_id(0)

    @pl.when(i == 0)
    def _():
        wb_ref[...] = w_ref[...].astype(BF16)

    off, length = _seq_tile_position(i)
    buf_ref[0:HALO, :] = jnp.where(off == 0, 0.0, zp_ref[...])
    buf_ref[HALO:HALO + SEQ_TILE, :] = zm_ref[...]
    buf_ref[HALO + SEQ_TILE:, :] = jnp.where(off + SEQ_TILE == length, 0.0, zn_ref[...])

    rows = 64
    for g, win in enumerate(POOL_WINDOWS):
        lo, hi = win // 2, win - win // 2 - 1
        cols = slice(g * POOL_GROUP, (g + 1) * POOL_GROUP)
        w_g = wb_ref[g]
        for rc in range(SEQ_TILE // rows):
            r0 = rc * rows
            acc = buf_ref[HALO + r0 - lo:HALO + r0 - lo + rows, cols]
            for s in range(-lo + 1, hi + 1):
                acc = acc + buf_ref[HALO + r0 + s:HALO + r0 + s + rows, cols]
            t = off + r0 + lax.broadcasted_iota(jnp.int32, (rows, 1), 0)
            count = (jnp.minimum(t + hi + 1, length) - jnp.maximum(t - lo, 0)).astype(F32)
            mixed = (acc / count - zm_ref[r0:r0 + rows, cols]).astype(BF16)
            y = jnp.dot(mixed, w_g, preferred_element_type=F32) * s_ref[:, cols]
            o_ref[r0:r0 + rows, cols] = y.astype(o_ref.dtype)


def _pool_mix(z, pool_w, pool_scale):
    main, prev, nxt = _halo_specs(POOL_WIDTH, 0)
    n_g = len(POOL_WINDOWS)
    vmem = (2 * (SEQ_TILE + 2 * HALO) * POOL_WIDTH * 4 + (SEQ_TILE + 2 * HALO) * POOL_WIDTH * 4
            + n_g * POOL_GROUP * POOL_GROUP * (2 * 4 + 2) + 2 * SEQ_TILE * POOL_WIDTH * 2)
    return pl.pallas_call(
        _pool_kernel,
        out_shape=jax.ShapeDtypeStruct((M_TOTAL, POOL_WIDTH), BF16),
        grid=(M_TOTAL // SEQ_TILE,),
        in_specs=[
            main, prev, nxt,
            pl.BlockSpec((n_g, POOL_GROUP, POOL_GROUP), lambda i: (0, 0, 0)),
            pl.BlockSpec((1, POOL_WIDTH), lambda i: (0, 0)),
        ],
        out_specs=pl.BlockSpec((SEQ_TILE, POOL_WIDTH), lambda i: (i, 0)),
        scratch_shapes=[
            pltpu.VMEM((SEQ_TILE + 2 * HALO, POOL_WIDTH), F32),
            pltpu.VMEM((n_g, POOL_GROUP, POOL_GROUP), BF16),
        ],
        compiler_params=_params(vmem, 1),
        name="pool_mix",
    )(z, z, z, pool_w, pool_scale.reshape(1, POOL_WIDTH))


def _conv_kernel(am_ref, ap_ref, an_ref, gm_ref, gp_ref, gn_ref, w_ref, lg_ref, lb_ref, o_ref, buf_ref, y_ref):
    i = pl.program_id(0)
    off, length = _seq_tile_position(i)

    def glu(a_ref, g_ref):
        return a_ref[...] * jax.nn.sigmoid(g_ref[...])

    buf_ref[0:HALO, :] = jnp.where(off == 0, 0.0, glu(ap_ref, gp_ref))
    buf_ref[HALO:HALO + SEQ_TILE, :] = glu(am_ref, gm_ref)
    buf_ref[HALO + SEQ_TILE:, :] = jnp.where(off + SEQ_TILE == length, 0.0, glu(an_ref, gn_ref))

    rows = 128
    first_tap_row = HALO - CONV_K // 2

    def channel_tile(c, carry):
        cols = pl.ds(pl.multiple_of(c * V7X_LANES, V7X_LANES), V7X_LANES)
        taps = [w_ref[k:k + 1, cols] for k in range(CONV_K)]
        for rc in range(SEQ_TILE // rows):
            r0 = first_tap_row + rc * rows
            acc = buf_ref[r0:r0 + rows, cols] * taps[0]
            for k in range(1, CONV_K):
                acc = acc + buf_ref[r0 + k:r0 + k + rows, cols] * taps[k]
            y_ref[rc * rows:(rc + 1) * rows, cols] = acc
        return carry

    lax.fori_loop(0, CONV_WIDTH // V7X_LANES, channel_tile, 0)

    ln_rows = 64
    for rc in range(SEQ_TILE // ln_rows):
        rs = slice(rc * ln_rows, (rc + 1) * ln_rows)
        y = y_ref[rs, :]
        mu = jnp.mean(y, axis=-1, keepdims=True)
        d = y - mu
        var = jnp.mean(d * d, axis=-1, keepdims=True)
        yn = d * lax.rsqrt(var + LN_EPS) * lg_ref[...] + lb_ref[...]
        o_ref[rs, :] = (yn * jax.nn.sigmoid(yn)).astype(o_ref.dtype)


def _conv_module(z, conv_dw, ln_g, ln_b):
    am, ap, an = _halo_specs(CONV_WIDTH, 1)
    gm, gp, gn = _halo_specs(CONV_WIDTH, 2)
    vec = pl.BlockSpec((1, CONV_WIDTH), lambda i: (0, 0))
    tile_rows = SEQ_TILE + 2 * HALO
    vmem = (2 * 2 * tile_rows * CONV_WIDTH * 4 + tile_rows * CONV_WIDTH * 4 + SEQ_TILE * CONV_WIDTH * 4
            + 2 * SEQ_TILE * CONV_WIDTH * 2 + 4 * SEQ_TILE * CONV_WIDTH * 4)
    return pl.pallas_call(
        _conv_kernel,
        out_shape=jax.ShapeDtypeStruct((M_TOTAL, CONV_WIDTH), BF16),
        grid=(M_TOTAL // SEQ_TILE,),
        in_specs=[am, ap, an, gm, gp, gn, pl.BlockSpec((CONV_K, CONV_WIDTH), lambda i: (0, 0)), vec, vec],
        out_specs=pl.BlockSpec((SEQ_TILE, CONV_WIDTH), lambda i: (i, 0)),
        scratch_shapes=[
            pltpu.VMEM((tile_rows, CONV_WIDTH), F32),
            pltpu.VMEM((SEQ_TILE, CONV_WIDTH), F32),
        ],
        compiler_params=_params(vmem, 1),
        name="conv_module",
    )(z, z, z, z, z, z, conv_dw, ln_g.reshape(1, CONV_WIDTH), ln_b.reshape(1, CONV_WIDTH))


_NT_DIMS = (((1,), (1,)), ((), ()))
ATTN_SCALE = HEAD_DIM ** -0.5
SOFTMAX_ROWS = 32


def _softmax_numerators(s_ref, e_ref, l_ref, n_rows, col0, n_cols):
    for c in range(n_rows // SOFTMAX_ROWS):
        rows = slice(c * SOFTMAX_ROWS, (c + 1) * SOFTMAX_ROWS)
        s = s_ref[rows, col0:col0 + n_cols]
        e = jnp.exp(s - jnp.max(s, axis=-1, keepdims=True))
        l_ref[rows, col0 // n_cols * V7X_LANES:(col0 // n_cols + 1) * V7X_LANES] = jnp.broadcast_to(
            jnp.sum(e, axis=-1, keepdims=True), (SOFTMAX_ROWS, V7X_LANES))
        e_ref[rows, col0:col0 + n_cols] = e.astype(BF16)


def _ctx_attn_kernel(q_ref, k_ref, v_ref, o_ref, s_ref, e_ref, l_ref, *, heads):
    for h in range(heads):
        cols = slice(h * HEAD_DIM, (h + 1) * HEAD_DIM)
        q = q_ref[:, cols].astype(BF16)
        k = k_ref[:, cols].astype(BF16)
        s_ref[:, h * SEQ:(h + 1) * SEQ] = lax.dot_general(q, k, _NT_DIMS, preferred_element_type=F32) * ATTN_SCALE
    for h in range(heads):
        _softmax_numerators(s_ref, e_ref, l_ref, SEQ, h * SEQ, SEQ)
    for h in range(heads):
        cols = slice(h * HEAD_DIM, (h + 1) * HEAD_DIM)
        v = v_ref[:, cols].astype(BF16)
        o = jnp.dot(e_ref[:, h * SEQ:(h + 1) * SEQ], v, preferred_element_type=F32) / l_ref[:, cols]
        o_ref[:, cols] = o.astype(o_ref.dtype)


def _ctx_attention(qkv):
    heads = 8
    width = heads * HEAD_DIM
    sections = D_MODEL // width

    def spec(section):
        return pl.BlockSpec((SEQ, width), lambda b, g: (b, section * sections + g))

    vmem = 2 * 3 * SEQ * width * 4 + 2 * SEQ * width * 2 + SEQ * heads * SEQ * 6 + SEQ * width * 4
    return pl.pallas_call(
        functools.partial(_ctx_attn_kernel, heads=heads),
        out_shape=jax.ShapeDtypeStruct((M_TOTAL, D_MODEL), BF16),
        grid=(BATCH, sections),
        in_specs=[spec(0), spec(1), spec(2)],
        out_specs=pl.BlockSpec((SEQ, width), lambda b, g: (b, g)),
        scratch_shapes=[
            pltpu.VMEM((SEQ, heads * SEQ), F32),
            pltpu.VMEM((SEQ, heads * SEQ), BF16),
            pltpu.VMEM((SEQ, width), F32),
        ],
        compiler_params=_params(vmem, 2),
        name="ctx_attention",
    )(qkv, qkv, qkv)


def _window_start(r):
    return min(max(r - WIN_ROWS // 2, 0), GRID_ROWS - WIN_ROWS)


def _na_attn_kernel(q_ref, k_ref, v_ref, kc_ref, vc_ref, t2_ref, prev_ref, o_ref, s_ref, e_ref, l_ref):
    del prev_ref
    q = q_ref[...].astype(BF16)
    k = k_ref[...].astype(BF16)
    n_loc = N_LOCAL_KEYS

    s_ref[:, n_loc:] = lax.dot_general(q, kc_ref[...].astype(BF16), _NT_DIMS,
                                       preferred_element_type=F32) * ATTN_SCALE
    for r in range(GRID_ROWS):
        rows = slice(r * GRID_W, (r + 1) * GRID_W)
        ws = _window_start(r)
        s = lax.dot_general(q[rows], k[ws * GRID_W:ws * GRID_W + n_loc], _NT_DIMS,
                            preferred_element_type=F32) * ATTN_SCALE
        d0 = ws - r + WIN_ROWS - 1
        for m in range(n_loc // V7X_LANES):
            lanes = slice(m * V7X_LANES, (m + 1) * V7X_LANES)
            s_ref[rows, lanes] = s[:, lanes] + t2_ref[d0 + 2 * m]

    _softmax_numerators(s_ref, e_ref, l_ref, DEC_SEQ, 0, n_loc + PAST_LEN)

    v = v_ref[...].astype(BF16)
    o_c = jnp.dot(e_ref[:, n_loc:], vc_ref[...].astype(BF16), preferred_element_type=F32)
    for r in range(GRID_ROWS):
        rows = slice(r * GRID_W, (r + 1) * GRID_W)
        ws = _window_start(r)
        o = jnp.dot(e_ref[rows, :n_loc], v[ws * GRID_W:ws * GRID_W + n_loc], preferred_element_type=F32)
        o_ref[rows, :] = ((o + o_c[rows]) / l_ref[rows, :]).astype(o_ref.dtype)


def _na_attention(qkv, cache_k, cache_v, t2, attn_out):
    row0 = M_PROMPT // DEC_SEQ

    def spec(section):
        return pl.BlockSpec((DEC_SEQ, HEAD_DIM), lambda b, h: (row0 + b, section * N_HEADS + h))

    cache_spec = pl.BlockSpec((None, PAST_LEN, HEAD_DIM), lambda b, h: (b, 0, h))
    n_keys = N_LOCAL_KEYS + PAST_LEN
    n_t2 = N_REL_ROWS - 1
    vmem = (2 * 3 * DEC_SEQ * HEAD_DIM * 4 + 2 * 2 * PAST_LEN * HEAD_DIM * 4
            + 2 * n_t2 * GRID_W * V7X_LANES * 4 + 2 * DEC_SEQ * HEAD_DIM * 2
            + DEC_SEQ * n_keys * 6 + DEC_SEQ * V7X_LANES * 4 + 4 * DEC_SEQ * HEAD_DIM * 4)
    return pl.pallas_call(
        _na_attn_kernel,
        out_shape=jax.ShapeDtypeStruct((M_TOTAL, D_MODEL), BF16),
        grid=(DEC_BATCH, N_HEADS),
        in_specs=[
            spec(0), spec(1), spec(2), cache_spec, cache_spec,
            pl.BlockSpec((None, n_t2, GRID_W, V7X_LANES), lambda b, h: (h, 0, 0, 0)),
            pl.BlockSpec(memory_space=pl.ANY),
        ],
        out_specs=pl.BlockSpec((DEC_SEQ, HEAD_DIM), lambda b, h: (row0 + b, h)),
        scratch_shapes=[
            pltpu.VMEM((DEC_SEQ, n_keys), F32),
            pltpu.VMEM((DEC_SEQ, n_keys), BF16),
            pltpu.VMEM((DEC_SEQ, V7X_LANES), F32),
        ],
        input_output_aliases={6: 0},
        compiler_params=_params(vmem, 2),
        name="na_attention",
    )(qkv, qkv, qkv, cache_k, cache_v, t2, attn_out)


def _na_bias_table(rpb):
    qc = np.arange(GRID_W)[:, None]
    kc = np.arange(GRID_W)[None, :]
    cstart = np.clip(qc - WIN_COLS // 2, 0, GRID_W - WIN_COLS)
    col_ok = (kc >= cstart) & (kc < cstart + WIN_COLS)
    dc = np.clip(kc - qc, -(WIN_COLS - 1), WIN_COLS - 1) + WIN_COLS - 1
    select = (dc.reshape(-1)[None, :] == np.arange(2 * WIN_COLS - 1)[:, None]).astype(np.float32)
    t = jnp.einsum("hrd,dx->hrx", rpb, jnp.asarray(select), precision=lax.Precision.HIGHEST)
    t = t.reshape(N_HEADS, N_REL_ROWS, GRID_W, GRID_W)
    t = jnp.where(jnp.asarray(col_ok)[None, None], t, -jnp.inf)
    return jnp.concatenate([t[:, :-1], t[:, 1:]], axis=-1)


def kernel(x_prompt, x_sample, cache_k_na, cache_v_na, c, c_ctx, ada_w, ada_b, norm1_g, norm2_g, ffn_w1, ffn_w3,
           ffn_w2, pc_w_in, pool_w, pool_scale, conv_dw, conv_ln_g, conv_ln_b, pc_w_out, na_w_qkv, na_q_g, na_k_g,
           na_rpb, na_w_o):
    x_in = (x_prompt.reshape(M_PROMPT, D_MODEL), x_sample.reshape(M_SAMPLE, D_MODEL))
    cond = jnp.concatenate([c_ctx[None], c, jnp.zeros((N_COND - 1 - DEC_BATCH, D_MODEL), F32)], axis=0)
    mod = _ada_mod(cond, ada_w, ada_b)
    mod = mod.reshape(DEPTH, N_COND, 6, D_MODEL).transpose(0, 2, 1, 3).reshape(DEPTH, 6, N_COND, 1, D_MODEL)

    def ffn(x, layer, split_out):
        h = _norm_mod(x, norm2_g[layer], mod[layer, 4], mod[layer, 3])
        hidden = _mm_swiglu(h, ffn_w1, ffn_w3, layer, tm=1024, tn=256)
        half = D_FF // 2
        part = _mm_plain(hidden, ffn_w2, layer, tm=512, tn=512, kb=half, kblk=0, out_dtype=F32)
        return _mm_res([hidden], [ffn_w2], layer, [1], [1], half, x, mod[layer, 5], part, tm=512, tn=512,
                       split_out=split_out)

    h = _norm_mod(x_in, norm1_g[0], mod[0, 1], mod[0, 0])
    z = _mm_plain(h, pc_w_in, 0, tm=1024, tn=512, kb=D_MODEL, kblk=0, out_dtype=F32)
    y_pool = _pool_mix(z, pool_w[0], pool_scale[0])
    y_conv = _conv_module(z, conv_dw[0], conv_ln_g[0], conv_ln_b[0])
    x = _mm_res([y_pool, y_conv], [pc_w_out, pc_w_out], 0, [0, 0], [0, 1], POOL_WIDTH, x_in, mod[0, 2],
                tm=1024, tn=512)
    x = ffn(x, 0, False)

    h = _norm_mod(x, norm1_g[1], mod[1, 1], mod[1, 0])
    gains = jnp.stack([na_q_g[0], na_k_g[0]]).reshape(2, 1, HEAD_DIM)
    qkv = _mm_qkv(h, na_w_qkv, gains, tm=1024, tn=512)
    new_k = qkv[:M_PROMPT, D_MODEL:2 * D_MODEL].reshape(BATCH, 1, SEQ, N_HEADS, HEAD_DIM)
    new_v = qkv[:M_PROMPT, 2 * D_MODEL:].reshape(BATCH, 1, SEQ, N_HEADS, HEAD_DIM)
    attn = _ctx_attention(qkv)
    attn = _na_attention(qkv, cache_k_na.reshape(DEC_BATCH, PAST_LEN, D_MODEL),
                         cache_v_na.reshape(DEC_BATCH, PAST_LEN, D_MODEL), _na_bias_table(na_rpb[0]), attn)
    x = _mm_res([attn], [na_w_o], 0, [0], [0], D_MODEL, x, mod[1, 2], tm=1024, tn=512)
    y_prompt, y_sample = ffn(x, 1, True)

    return (y_prompt.reshape(BATCH, SEQ, D_MODEL), y_sample.reshape(DEC_BATCH, DEC_SEQ, D_MODEL), new_k, new_v)
```

```python
import functools

import numpy as np
import jax
import jax.numpy as jnp
from jax import lax
from jax.experimental import pallas as pl
from jax.experimental.pallas import tpu as pltpu

F32 = jnp.float32
BF16 = jnp.bfloat16

D_MODEL = 4096
BATCH = 16
SEQ = 256
DEPTH = 2
DEC_BATCH = 8
DEC_SEQ = 1024
PAST_LEN = 512
GRID_W = 64
HEAD_DIM = 128
N_HEADS = D_MODEL // HEAD_DIM
POOL_WIDTH = D_MODEL // 2
CONV_WIDTH = D_MODEL // 2
POOL_WINDOWS = (2, 4, 8, 16)
POOL_GROUP = POOL_WIDTH // len(POOL_WINDOWS)
CONV_K = 31
WIN_ROWS = 8
WIN_COLS = 16
D_FF = 11008
RMS_EPS = 1e-6
LN_EPS = 1e-5

M_PROMPT = BATCH * SEQ
M_SAMPLE = DEC_BATCH * DEC_SEQ
M_TOTAL = M_PROMPT + M_SAMPLE
N_COND = 16
GRID_ROWS = DEC_SEQ // GRID_W
N_LOCAL_KEYS = WIN_ROWS * GRID_W
N_REL_ROWS = 2 * WIN_ROWS - 1

V7X_VMEM_BYTES = 64 * 1024 * 1024
V7X_LANES = 128
V7X_SUBLANES = 8
COMPILER_SCRATCH_BYTES = 6 * 1024 * 1024

HALO = 16
SEQ_TILE = 256


def _params(vmem_bytes, n_axes):
    limit = min(int(vmem_bytes) + COMPILER_SCRATCH_BYTES, V7X_VMEM_BYTES - 2 * 1024 * 1024)
    return pltpu.CompilerParams(dimension_semantics=("arbitrary",) * n_axes, vmem_limit_bytes=limit)


def _cond_row(row_start):
    return jnp.maximum(row_start // DEC_SEQ - (M_PROMPT // DEC_SEQ - 1), 0)


def _split_specs(tm, tn, n_grid_axes):
    n_p = M_PROMPT // tm
    if n_grid_axes == 1:
        return (pl.BlockSpec((tm, tn), lambda i: (jnp.minimum(i, n_p - 1), 0)),
                pl.BlockSpec((tm, tn), lambda i: (jnp.maximum(i - n_p, 0), 0)))
    return (pl.BlockSpec((tm, tn), lambda j, i: (jnp.minimum(i, n_p - 1), j)),
            pl.BlockSpec((tm, tn), lambda j, i: (jnp.maximum(i - n_p, 0), j)))


def _ada_kernel(c_ref, w_ref, b_ref, o_ref):
    c = c_ref[...]
    s = (c * jax.nn.sigmoid(c)).astype(BF16)
    o_ref[...] = jnp.dot(s, w_ref[...].astype(BF16), preferred_element_type=F32) + b_ref[...]


def _ada_mod(cond, ada_w, ada_b):
    tn = 512
    n = 6 * D_MODEL
    vmem = 2 * D_MODEL * tn * 4 + D_MODEL * tn * 2 + 2 * N_COND * D_MODEL * 4
    return pl.pallas_call(
        _ada_kernel,
        out_shape=jax.ShapeDtypeStruct((DEPTH, N_COND, n), F32),
        grid=(DEPTH, n // tn),
        in_specs=[
            pl.BlockSpec((N_COND, D_MODEL), lambda l, j: (0, 0)),
            pl.BlockSpec((None, D_MODEL, tn), lambda l, j: (l, 0, j)),
            pl.BlockSpec((None, 1, tn), lambda l, j: (l, 0, j)),
        ],
        out_specs=pl.BlockSpec((None, N_COND, tn), lambda l, j: (l, 0, j)),
        compiler_params=_params(vmem, 2),
        name="ada_mod",
    )(cond, ada_w, ada_b.reshape(DEPTH, 1, n))


NORM_ROWS = 16


def _norm_mod_rows(x_ref, r_ref, gain_ref, shift_ref, o_ref):
    n_chunks = x_ref.shape[0] // NORM_ROWS
    n_tiles = x_ref.shape[1] // V7X_LANES
    for c in range(n_chunks):
        rows = slice(c * NORM_ROWS, (c + 1) * NORM_ROWS)
        x = x_ref[rows, :]
        ms = jnp.mean(x * x, axis=-1, keepdims=True)
        r_ref[rows, :] = jnp.broadcast_to(lax.rsqrt(ms + RMS_EPS), (NORM_ROWS, V7X_LANES))
    for c in range(n_chunks):
        rows = slice(c * NORM_ROWS, (c + 1) * NORM_ROWS)
        r = r_ref[rows, :]
        for t in range(n_tiles):
            lanes = slice(t * V7X_LANES, (t + 1) * V7X_LANES)
            y = x_ref[rows, lanes] * r * gain_ref[:, lanes] + shift_ref[:, lanes]
            o_ref[rows, lanes] = y.astype(o_ref.dtype)


def _norm_mod_prepare(g_ref, sc_ref, sh_ref, gain_ref, shift_ref):
    gain_ref[...] = jnp.broadcast_to(g_ref[...] * (1.0 + sc_ref[...]), gain_ref.shape)
    shift_ref[...] = jnp.broadcast_to(sh_ref[...], shift_ref.shape)


def _norm_mod_kernel(x_ref, g_ref, sc_ref, sh_ref, o_ref, r_ref, gain_ref, shift_ref):
    _norm_mod_prepare(g_ref, sc_ref, sh_ref, gain_ref, shift_ref)
    _norm_mod_rows(x_ref, r_ref, gain_ref, shift_ref, o_ref)


def _norm_mod_split_kernel(xp_ref, xs_ref, g_ref, sc_ref, sh_ref, o_ref, r_ref, gain_ref, shift_ref, *, n_p):
    _norm_mod_prepare(g_ref, sc_ref, sh_ref, gain_ref, shift_ref)

    @pl.when(pl.program_id(0) < n_p)
    def _():
        _norm_mod_rows(xp_ref, r_ref, gain_ref, shift_ref, o_ref)

    @pl.when(pl.program_id(0) >= n_p)
    def _():
        _norm_mod_rows(xs_ref, r_ref, gain_ref, shift_ref, o_ref)


def _norm_mod(x, g, scale, shift):
    tm = 256
    mod_spec = pl.BlockSpec((None, 1, D_MODEL), lambda i: (_cond_row(i * tm), 0, 0))
    if isinstance(x, tuple):
        body = functools.partial(_norm_mod_split_kernel, n_p=M_PROMPT // tm)
        x_specs = list(_split_specs(tm, D_MODEL, 1))
        xs = list(x)
    else:
        body = _norm_mod_kernel
        x_specs = [pl.BlockSpec((tm, D_MODEL), lambda i: (i, 0))]
        xs = [x]
    vmem = 2 * tm * D_MODEL * (4 * len(xs) + 2) + 4 * tm * D_MODEL * 4
    return pl.pallas_call(
        body,
        out_shape=jax.ShapeDtypeStruct((M_TOTAL, D_MODEL), BF16),
        grid=(M_TOTAL // tm,),
        in_specs=x_specs + [pl.BlockSpec((1, D_MODEL), lambda i: (0, 0)), mod_spec, mod_spec],
        out_specs=pl.BlockSpec((tm, D_MODEL), lambda i: (i, 0)),
        scratch_shapes=[
            pltpu.VMEM((tm, V7X_LANES), F32),
            pltpu.VMEM((NORM_ROWS, D_MODEL), F32),
            pltpu.VMEM((NORM_ROWS, D_MODEL), F32),
        ],
        compiler_params=_params(vmem, 1),
        name="norm_mod",
    )(*xs, g.reshape(1, D_MODEL), scale, shift)


def _cast_weights_once(w_refs, wb_refs):
    @pl.when(pl.program_id(1) == 0)
    def _():
        for w_ref, wb_ref in zip(w_refs, wb_refs):
            wb_ref[...] = w_ref[...].astype(BF16)


def _w_spec(kb, tn, layer, kblk):
    return pl.BlockSpec((None, kb, tn), lambda j, i: (layer, kblk, j))


def _mm_plain_kernel(x_ref, w_ref, o_ref, wb_ref):
    _cast_weights_once([w_ref], [wb_ref])
    o_ref[...] = jnp.dot(x_ref[...], wb_ref[...], preferred_element_type=F32).astype(o_ref.dtype)


def _mm_res_kernel(*refs, n_lhs, split_x, has_partial, n_res, n_out, n_p):
    refs = list(refs)
    x_refs = [refs.pop(0) for _ in range(2 if split_x else n_lhs)]
    w_refs = [refs.pop(0) for _ in range(n_lhs)]
    p_ref = refs.pop(0) if has_partial else None
    res_refs = [refs.pop(0) for _ in range(n_res)]
    gate_ref = refs.pop(0)
    out_refs = [refs.pop(0) for _ in range(n_out)]
    wb_refs = refs
    _cast_weights_once(w_refs, wb_refs)
    is_prompt = pl.program_id(1) < n_p

    def body(lhs_refs, side):
        acc = jnp.dot(lhs_refs[0][...], wb_refs[0][...], preferred_element_type=F32)
        for x_ref, wb_ref in zip(lhs_refs[1:], wb_refs[1:]):
            acc = acc + jnp.dot(x_ref[...], wb_ref[...], preferred_element_type=F32)
        if has_partial:
            acc = acc + p_ref[...]
        if n_res == 1:
            res = res_refs[0][...]
        elif side is None:
            res = jnp.where(is_prompt, res_refs[0][...], res_refs[1][...])
        else:
            res = res_refs[side][...]
        y = res + gate_ref[...] * acc
        if n_out == 1:
            out_refs[0][...] = y
        elif side is None:
            @pl.when(is_prompt)
            def _():
                out_refs[0][...] = y

            @pl.when(jnp.logical_not(is_prompt))
            def _():
                out_refs[1][...] = y
        else:
            out_refs[side][...] = y

    if split_x:
        @pl.when(is_prompt)
        def _():
            body([x_refs[0]], 0)

        @pl.when(jnp.logical_not(is_prompt))
        def _():
            body([x_refs[1]], 1)
    else:
        body(x_refs, None)


def _mm_swiglu_kernel(x_ref, w1_ref, w3_ref, o_ref, wb1_ref, wb3_ref):
    _cast_weights_once([w1_ref, w3_ref], [wb1_ref, wb3_ref])
    x = x_ref[...]
    a = jnp.dot(x, wb1_ref[...], preferred_element_type=F32)
    b = jnp.dot(x, wb3_ref[...], preferred_element_type=F32)
    o_ref[...] = (a * jax.nn.sigmoid(a) * b).astype(o_ref.dtype)


def _mm_qkv_kernel(x_ref, w_ref, g_ref, o_ref, kv_ref, wb_ref, acc_ref, *, tn, n_p):
    _cast_weights_once([w_ref], [wb_ref])
    j, i = pl.program_id(0), pl.program_id(1)
    tiles_per_section = D_MODEL // tn
    is_v = j >= 2 * tiles_per_section
    wants_kv = jnp.logical_and(j >= tiles_per_section, i < n_p)

    def product():
        return jnp.dot(x_ref[...], wb_ref[...], preferred_element_type=F32)

    def normed(write_kv):
        acc_ref[...] = product()
        for h in range(tn // HEAD_DIM):
            cols = slice(h * HEAD_DIM, (h + 1) * HEAD_DIM)
            z = acc_ref[:, cols]
            ms = jnp.mean(z * z, axis=-1, keepdims=True)
            zn = z * lax.rsqrt(ms + RMS_EPS) * g_ref[...]
            o_ref[:, cols] = zn.astype(o_ref.dtype)
            if write_kv:
                kv_ref[:, cols] = zn

    def plain(write_kv):
        z = product()
        o_ref[...] = z.astype(o_ref.dtype)
        if write_kv:
            kv_ref[...] = z

    not_v = jnp.logical_not(is_v)
    no_kv = jnp.logical_not(wants_kv)
    pl.when(jnp.logical_and(not_v, no_kv))(lambda: normed(False))
    pl.when(jnp.logical_and(not_v, wants_kv))(lambda: normed(True))
    pl.when(jnp.logical_and(is_v, no_kv))(lambda: plain(False))
    pl.when(jnp.logical_and(is_v, wants_kv))(lambda: plain(True))


def _mm_vmem(tm, tn, kb, n_lhs_bytes, n_w, out_bytes, extra_tiles):
    x = 2 * tm * n_lhs_bytes
    w = n_w * kb * tn * (2 * 4 + 2)
    o = 2 * tm * tn * out_bytes
    e = extra_tiles * 2 * tm * tn * 4
    return x + w + o + e + 2 * tm * tn * 4


def _mm_plain(x, w, layer, *, tm, tn, kb, kblk, out_dtype):
    m = x.shape[0]
    n = w.shape[2]
    vmem = _mm_vmem(tm, tn, kb, kb * 2, 1, jnp.dtype(out_dtype).itemsize, 0)
    return pl.pallas_call(
        _mm_plain_kernel,
        out_shape=jax.ShapeDtypeStruct((m, n), out_dtype),
        grid=(n // tn, m // tm),
        in_specs=[pl.BlockSpec((tm, kb), lambda j, i: (i, kblk)), _w_spec(kb, tn, layer, kblk)],
        out_specs=pl.BlockSpec((tm, tn), lambda j, i: (i, j)),
        scratch_shapes=[pltpu.VMEM((kb, tn), BF16)],
        compiler_params=_params(vmem, 2),
        name="mm_plain",
    )(x, w)


def _mm_res(xs, ws, layer, xblks, wblks, kb, res, gate, partial=None, *, tm, tn, split_x=False, split_out=False):
    n = ws[0].shape[2]
    n_lhs = len(ws)
    n_p = M_PROMPT // tm
    tile = pl.BlockSpec((tm, tn), lambda j, i: (i, j))
    if split_x:
        kk0 = xblks[0]
        in_specs = [pl.BlockSpec((tm, kb), lambda j, i: (jnp.minimum(i, n_p - 1), kk0)),
                    pl.BlockSpec((tm, kb), lambda j, i: (jnp.maximum(i - n_p, 0), kk0))]
    else:
        in_specs = [pl.BlockSpec((tm, kb), functools.partial(lambda j, i, kk: (i, kk), kk=kk)) for kk in xblks]
    in_specs += [_w_spec(kb, tn, layer, kk) for kk in wblks]
    args = list(xs) + list(ws)
    if partial is not None:
        in_specs.append(tile)
        args.append(partial)
    res_list = list(res) if isinstance(res, tuple) else [res]
    in_specs += list(_split_specs(tm, tn, 2)) if len(res_list) == 2 else [tile]
    in_specs.append(pl.BlockSpec((None, 1, tn), lambda j, i: (_cond_row(i * tm), 0, j)))
    args += res_list + [gate]
    if split_out:
        out_shape = (jax.ShapeDtypeStruct((M_PROMPT, n), F32), jax.ShapeDtypeStruct((M_SAMPLE, n), F32))
        out_specs = _split_specs(tm, tn, 2)
    else:
        out_shape = jax.ShapeDtypeStruct((M_TOTAL, n), F32)
        out_specs = tile
    n_out = 2 if split_out else 1
    vmem = _mm_vmem(tm, tn, kb, len(xs) * kb * 2, n_lhs, 4 * n_out, len(res_list) + (partial is not None))
    return pl.pallas_call(
        functools.partial(_mm_res_kernel, n_lhs=n_lhs, split_x=split_x, has_partial=partial is not None,
                          n_res=len(res_list), n_out=n_out, n_p=n_p),
        out_shape=out_shape,
        grid=(n // tn, M_TOTAL // tm),
        in_specs=in_specs,
        out_specs=out_specs,
        scratch_shapes=[pltpu.VMEM((kb, tn), BF16) for _ in range(n_lhs)],
        compiler_params=_params(vmem, 2),
        name="mm_res",
    )(*args)


def _mm_swiglu(x, w1, w3, layer, *, tm, tn):
    m, k = x.shape
    n = w1.shape[2]
    vmem = _mm_vmem(tm, tn, k, k * 2, 2, 2, 2)
    return pl.pallas_call(
        _mm_swiglu_kernel,
        out_shape=jax.ShapeDtypeStruct((m, n), BF16),
        grid=(n // tn, m // tm),
        in_specs=[pl.BlockSpec((tm, k), lambda j, i: (i, 0)), _w_spec(k, tn, layer, 0), _w_spec(k, tn, layer, 0)],
        out_specs=pl.BlockSpec((tm, tn), lambda j, i: (i, j)),
        scratch_shapes=[pltpu.VMEM((k, tn), BF16), pltpu.VMEM((k, tn), BF16)],
        compiler_params=_params(vmem, 2),
        name="mm_swiglu",
    )(x, w1, w3)


def _mm_qkv(x, w, gains, *, tm, tn):
    m, k = x.shape
    n = w.shape[2]
    tiles_per_section = D_MODEL // tn
    n_p = M_PROMPT // tm

    def kv_index(j, i):
        is_kv = j >= tiles_per_section
        return jnp.where(is_kv, jnp.minimum(i, n_p - 1), 0), jnp.maximum(j - tiles_per_section, 0)

    vmem = _mm_vmem(tm, tn, k, k * 2, 1, 2, 2)
    return pl.pallas_call(
        functools.partial(_mm_qkv_kernel, tn=tn, n_p=n_p),
        out_shape=(jax.ShapeDtypeStruct((m, n), BF16), jax.ShapeDtypeStruct((M_PROMPT, 2 * D_MODEL), F32)),
        grid=(n // tn, m // tm),
        in_specs=[
            pl.BlockSpec((tm, k), lambda j, i: (i, 0)),
            _w_spec(k, tn, 0, 0),
            pl.BlockSpec((None, 1, HEAD_DIM), lambda j, i: (jnp.minimum(j // tiles_per_section, 1), 0, 0)),
        ],
        out_specs=(pl.BlockSpec((tm, tn), lambda j, i: (i, j)), pl.BlockSpec((tm, tn), kv_index)),
        scratch_shapes=[pltpu.VMEM((k, tn), BF16), pltpu.VMEM((tm, tn), F32)],
        compiler_params=_params(vmem, 2),
        name="mm_qkv",
    )(x, w, gains)


def _seq_tile_position(i):
    is_prompt = i < M_PROMPT // SEQ_TILE
    tiles_per_seq = DEC_SEQ // SEQ_TILE
    off = jnp.where(is_prompt, 0, (jnp.maximum(i - M_PROMPT // SEQ_TILE, 0) % tiles_per_seq) * SEQ_TILE)
    length = jnp.where(is_prompt, SEQ, DEC_SEQ)
    return off, length


def _halo_specs(width, col_block):
    n_halo_blocks = M_TOTAL // HALO
    per_tile = SEQ_TILE // HALO
    main = pl.BlockSpec((SEQ_TILE, width), lambda i: (i, col_block))
    prev = pl.BlockSpec((HALO, width), lambda i: (jnp.maximum(i * per_tile - 1, 0), col_block))
    nxt = pl.BlockSpec((HALO, width), lambda i: (jnp.minimum((i + 1) * per_tile, n_halo_blocks - 1), col_block))
    return main, prev, nxt


def _pool_kernel(zm_ref, zp_ref, zn_ref, w_ref, s_ref, o_ref, buf_ref, wb_ref):
    i = pl.program_id(0)

    @pl.when(i == 0)
    def _():
        wb_ref[...] = w_ref[...].astype(BF16)

    off, length = _seq_tile_position(i)
    buf_ref[0:HALO, :] = jnp.where(off == 0, 0.0, zp_ref[...])
    buf_ref[HALO:HALO + SEQ_TILE, :] = zm_ref[...]
    buf_ref[HALO + SEQ_TILE:, :] = jnp.where(off + SEQ_TILE == length, 0.0, zn_ref[...])

    rows = 64
    for g, win in enumerate(POOL_WINDOWS):
        lo, hi = win // 2, win - win // 2 - 1
        cols = slice(g * POOL_GROUP, (g + 1) * POOL_GROUP)
        w_g = wb_ref[g]
        for rc in range(SEQ_TILE // rows):
            r0 = rc * rows
            acc = buf_ref[HALO + r0 - lo:HALO + r0 - lo + rows, cols]
            for s in range(-lo + 1, hi + 1):
                acc = acc + buf_ref[HALO + r0 + s:HALO + r0 + s + rows, cols]
            t = off + r0 + lax.broadcasted_iota(jnp.int32, (rows, 1), 0)
            count = (jnp.minimum(t + hi + 1, length) - jnp.maximum(t - lo, 0)).astype(F32)
            mixed = (acc / count - zm_ref[r0:r0 + rows, cols]).astype(BF16)
            y = jnp.dot(mixed, w_g, preferred_element_type=F32) * s_ref[:, cols]
            o_ref[r0:r0 + rows, cols] = y.astype(o_ref.dtype)


def _pool_mix(z, pool_w, pool_scale):
    main, prev, nxt = _halo_specs(POOL_WIDTH, 0)
    n_g = len(POOL_WINDOWS)
    vmem = (2 * (SEQ_TILE + 2 * HALO) * POOL_WIDTH * 4 + (SEQ_TILE + 2 * HALO) * POOL_WIDTH * 4
            + n_g * POOL_GROUP * POOL_GROUP * (2 * 4 + 2) + 2 * SEQ_TILE * POOL_WIDTH * 2)
    return pl.pallas_call(
        _pool_kernel,
        out_shape=jax.ShapeDtypeStruct((M_TOTAL, POOL_WIDTH), BF16),
        grid=(M_TOTAL // SEQ_TILE,),
        in_specs=[
            main, prev, nxt,
            pl.BlockSpec((n_g, POOL_GROUP, POOL_GROUP), lambda i: (0, 0, 0)),
            pl.BlockSpec((1, POOL_WIDTH), lambda i: (0, 0)),
        ],
        out_specs=pl.BlockSpec((SEQ_TILE, POOL_WIDTH), lambda i: (i, 0)),
        scratch_shapes=[
            pltpu.VMEM((SEQ_TILE + 2 * HALO, POOL_WIDTH), F32),
            pltpu.VMEM((n_g, POOL_GROUP, POOL_GROUP), BF16),
        ],
        compiler_params=_params(vmem, 1),
        name="pool_mix",
    )(z, z, z, pool_w, pool_scale.reshape(1, POOL_WIDTH))


def _conv_kernel(am_ref, ap_ref, an_ref, gm_ref, gp_ref, gn_ref, w_ref, lg_ref, lb_ref, o_ref, buf_ref, y_ref,
                 win_ref):
    i = pl.program_id(0)
    off, length = _seq_tile_position(i)

    def glu(a_ref, g_ref):
        return a_ref[...] * jax.nn.sigmoid(g_ref[...])

    buf_ref[0:HALO, :] = jnp.where(off == 0, 0.0, glu(ap_ref, gp_ref))
    buf_ref[HALO:HALO + SEQ_TILE, :] = glu(am_ref, gm_ref)
    buf_ref[HALO + SEQ_TILE:, :] = jnp.where(off + SEQ_TILE == length, 0.0, glu(an_ref, gn_ref))

    rows = 128
    first_tap_row = HALO - CONV_K // 2

    def channel_tile(c, carry):
        cols = pl.ds(pl.multiple_of(c * V7X_LANES, V7X_LANES), V7X_LANES)
        taps = [w_ref[k:k + 1, cols] for k in range(CONV_K)]
        for rc in range(SEQ_TILE // rows):
            acc = None
            for phase in range(V7X_SUBLANES):
                ks = [k for k in range(CONV_K) if (first_tap_row + k) % V7X_SUBLANES == phase]
                base = first_tap_row + ks[0] + rc * rows
                n_win = rows + V7X_SUBLANES * (len(ks) - 1)
                win_ref[phase, 0:n_win, :] = buf_ref[base:base + n_win, cols]
                for a, k in enumerate(ks):
                    term = win_ref[phase, V7X_SUBLANES * a:V7X_SUBLANES * a + rows, :] * taps[k]
                    acc = term if acc is None else acc + term
            y_ref[rc * rows:(rc + 1) * rows, cols] = acc
        return carry

    lax.fori_loop(0, CONV_WIDTH // V7X_LANES, channel_tile, 0)

    ln_rows = 64
    for rc in range(SEQ_TILE // ln_rows):
        rs = slice(rc * ln_rows, (rc + 1) * ln_rows)
        y = y_ref[rs, :]
        mu = jnp.mean(y, axis=-1, keepdims=True)
        d = y - mu
        var = jnp.mean(d * d, axis=-1, keepdims=True)
        yn = d * lax.rsqrt(var + LN_EPS) * lg_ref[...] + lb_ref[...]
        o_ref[rs, :] = (yn * jax.nn.sigmoid(yn)).astype(o_ref.dtype)


def _conv_module(z, conv_dw, ln_g, ln_b):
    am, ap, an = _halo_specs(CONV_WIDTH, 1)
    gm, gp, gn = _halo_specs(CONV_WIDTH, 2)
    vec = pl.BlockSpec((1, CONV_WIDTH), lambda i: (0, 0))
    tile_rows = SEQ_TILE + 2 * HALO
    vmem = (2 * 2 * tile_rows * CONV_WIDTH * 4 + tile_rows * CONV_WIDTH * 4 + SEQ_TILE * CONV_WIDTH * 4
            + 2 * SEQ_TILE * CONV_WIDTH * 2 + 4 * SEQ_TILE * CONV_WIDTH * 4)
    return pl.pallas_call(
        _conv_kernel,
        out_shape=jax.ShapeDtypeStruct((M_TOTAL, CONV_WIDTH), BF16),
        grid=(M_TOTAL // SEQ_TILE,),
        in_specs=[am, ap, an, gm, gp, gn, pl.BlockSpec((CONV_K, CONV_WIDTH), lambda i: (0, 0)), vec, vec],
        out_specs=pl.BlockSpec((SEQ_TILE, CONV_WIDTH), lambda i: (i, 0)),
        scratch_shapes=[
            pltpu.VMEM((tile_rows, CONV_WIDTH), F32),
            pltpu.VMEM((SEQ_TILE, CONV_WIDTH), F32),
            pltpu.VMEM((V7X_SUBLANES, 128 + V7X_SUBLANES * (-(-CONV_K // V7X_SUBLANES) - 1), V7X_LANES), F32),
        ],
        compiler_params=_params(vmem, 1),
        name="conv_module",
    )(z, z, z, z, z, z, conv_dw, ln_g.reshape(1, CONV_WIDTH), ln_b.reshape(1, CONV_WIDTH))


_NT_DIMS = (((1,), (1,)), ((), ()))
ATTN_SCALE = HEAD_DIM ** -0.5
SOFTMAX_ROWS = 32


def _softmax_numerators(s_ref, e_ref, l_ref, n_rows, col0, n_cols):
    for c in range(n_rows // SOFTMAX_ROWS):
        rows = slice(c * SOFTMAX_ROWS, (c + 1) * SOFTMAX_ROWS)
        s = s_ref[rows, col0:col0 + n_cols]
        e = jnp.exp(s - jnp.max(s, axis=-1, keepdims=True))
        l_ref[rows, col0 // n_cols * V7X_LANES:(col0 // n_cols + 1) * V7X_LANES] = jnp.broadcast_to(
            jnp.sum(e, axis=-1, keepdims=True), (SOFTMAX_ROWS, V7X_LANES))
        e_ref[rows, col0:col0 + n_cols] = e.astype(BF16)


def _ctx_attn_kernel(q_ref, k_ref, v_ref, o_ref, s_ref, e_ref, l_ref, *, heads):
    for h in range(heads):
        cols = slice(h * HEAD_DIM, (h + 1) * HEAD_DIM)
        s_ref[:, h * SEQ:(h + 1) * SEQ] = lax.dot_general(
            q_ref[:, cols], k_ref[:, cols], _NT_DIMS, preferred_element_type=F32) * ATTN_SCALE
    for h in range(heads):
        _softmax_numerators(s_ref, e_ref, l_ref, SEQ, h * SEQ, SEQ)
    for h in range(heads):
        cols = slice(h * HEAD_DIM, (h + 1) * HEAD_DIM)
        o = jnp.dot(e_ref[:, h * SEQ:(h + 1) * SEQ], v_ref[:, cols], preferred_element_type=F32) / l_ref[:, cols]
        o_ref[:, cols] = o.astype(o_ref.dtype)


def _ctx_attention(qkv):
    heads = 8
    width = heads * HEAD_DIM
    sections = D_MODEL // width

    def spec(section):
        return pl.BlockSpec((SEQ, width), lambda b, g: (b, section * sections + g))

    vmem = 2 * 4 * SEQ * width * 2 + SEQ * heads * SEQ * 6 + SEQ * width * 4 + 4 * SEQ * heads * SEQ * 4
    return pl.pallas_call(
        functools.partial(_ctx_attn_kernel, heads=heads),
        out_shape=jax.ShapeDtypeStruct((M_PROMPT, D_MODEL), BF16),
        grid=(BATCH, sections),
        in_specs=[spec(0), spec(1), spec(2)],
        out_specs=pl.BlockSpec((SEQ, width), lambda b, g: (b, g)),
        scratch_shapes=[
            pltpu.VMEM((SEQ, heads * SEQ), F32),
            pltpu.VMEM((SEQ, heads * SEQ), BF16),
            pltpu.VMEM((SEQ, width), F32),
        ],
        compiler_params=_params(vmem, 2),
        name="ctx_attention",
    )(qkv, qkv, qkv)


def _window_start(r):
    return min(max(r - WIN_ROWS // 2, 0), GRID_ROWS - WIN_ROWS)


def _na_attn_kernel(q_ref, k_ref, v_ref, kc_ref, vc_ref, t2_ref, o_ref, s_ref, e_ref, l_ref, *, heads):
    n_loc = N_LOCAL_KEYS

    def one_head(h, carry):
        cols = pl.ds(pl.multiple_of(h * HEAD_DIM, HEAD_DIM), HEAD_DIM)
        q = q_ref[:, cols]
        k = k_ref[:, cols]
        s_ref[:, n_loc:] = lax.dot_general(q, kc_ref[:, cols].astype(BF16), _NT_DIMS,
                                           preferred_element_type=F32) * ATTN_SCALE
        for r in range(GRID_ROWS):
            rows = slice(r * GRID_W, (r + 1) * GRID_W)
            ws = _window_start(r)
            s = lax.dot_general(q[rows], k[ws * GRID_W:ws * GRID_W + n_loc], _NT_DIMS,
                                preferred_element_type=F32) * ATTN_SCALE
            d0 = ws - r + WIN_ROWS - 1
            for m in range(n_loc // V7X_LANES):
                lanes = slice(m * V7X_LANES, (m + 1) * V7X_LANES)
                s_ref[rows, lanes] = s[:, lanes] + t2_ref[h, d0 + 2 * m]

        _softmax_numerators(s_ref, e_ref, l_ref, DEC_SEQ, 0, n_loc + PAST_LEN)

        v = v_ref[:, cols]
        o_c = jnp.dot(e_ref[:, n_loc:], vc_ref[:, cols].astype(BF16), preferred_element_type=F32)
        for r in range(GRID_ROWS):
            rows = slice(r * GRID_W, (r + 1) * GRID_W)
            ws = _window_start(r)
            o = jnp.dot(e_ref[rows, :n_loc], v[ws * GRID_W:ws * GRID_W + n_loc], preferred_element_type=F32)
            o_ref[rows, cols] = ((o + o_c[rows]) / l_ref[rows, :]).astype(o_ref.dtype)
        return carry

    lax.fori_loop(0, heads, one_head, 0)


def _na_attention(qkv, cache_k, cache_v, t2):
    heads = 4
    width = heads * HEAD_DIM
    sections = D_MODEL // width
    row0 = M_PROMPT // DEC_SEQ

    def spec(section):
        return pl.BlockSpec((DEC_SEQ, width), lambda b, g: (row0 + b, section * sections + g))

    cache_spec = pl.BlockSpec((None, PAST_LEN, width), lambda b, g: (b, 0, g))
    n_keys = N_LOCAL_KEYS + PAST_LEN
    n_t2 = N_REL_ROWS - 1
    vmem = (2 * 4 * DEC_SEQ * width * 2 + 2 * 2 * PAST_LEN * width * 4
            + 2 * heads * n_t2 * GRID_W * V7X_LANES * 4
            + DEC_SEQ * n_keys * 6 + DEC_SEQ * V7X_LANES * 4 + 6 * DEC_SEQ * HEAD_DIM * 4)
    return pl.pallas_call(
        functools.partial(_na_attn_kernel, heads=heads),
        out_shape=jax.ShapeDtypeStruct((M_SAMPLE, D_MODEL), BF16),
        grid=(DEC_BATCH, sections),
        in_specs=[
            spec(0), spec(1), spec(2), cache_spec, cache_spec,
            pl.BlockSpec((heads, n_t2, GRID_W, V7X_LANES), lambda b, g: (g, 0, 0, 0)),
        ],
        out_specs=pl.BlockSpec((DEC_SEQ, width), lambda b, g: (b, g)),
        scratch_shapes=[
            pltpu.VMEM((DEC_SEQ, n_keys), F32),
            pltpu.VMEM((DEC_SEQ, n_keys), BF16),
            pltpu.VMEM((DEC_SEQ, V7X_LANES), F32),
        ],
        compiler_params=_params(vmem, 2),
        name="na_attention",
    )(qkv, qkv, qkv, cache_k, cache_v, t2)


def _na_bias_table(rpb):
    qc = np.arange(GRID_W)[:, None]
    kc = np.arange(GRID_W)[None, :]
    cstart = np.clip(qc - WIN_COLS // 2, 0, GRID_W - WIN_COLS)
    col_ok = (kc >= cstart) & (kc < cstart + WIN_COLS)
    dc = np.clip(kc - qc, -(WIN_COLS - 1), WIN_COLS - 1) + WIN_COLS - 1
    select = (dc.reshape(-1)[None, :] == np.arange(2 * WIN_COLS - 1)[:, None]).astype(np.float32)
    t = jnp.einsum("hrd,dx->hrx", rpb, jnp.asarray(select), precision=lax.Precision.HIGHEST)
    t = t.reshape(N_HEADS, N_REL_ROWS, GRID_W, GRID_W)
    t = jnp.where(jnp.asarray(col_ok)[None, None], t, -jnp.inf)
    return jnp.concatenate([t[:, :-1], t[:, 1:]], axis=-1)


def kernel(x_prompt, x_sample, cache_k_na, cache_v_na, c, c_ctx, ada_w, ada_b, norm1_g, norm2_g, ffn_w1, ffn_w3,
           ffn_w2, pc_w_in, pool_w, pool_scale, conv_dw, conv_ln_g, conv_ln_b, pc_w_out, na_w_qkv, na_q_g, na_k_g,
           na_rpb, na_w_o):
    x_in = (x_prompt.reshape(M_PROMPT, D_MODEL), x_sample.reshape(M_SAMPLE, D_MODEL))
    cond = jnp.concatenate([c_ctx[None], c, jnp.zeros((N_COND - 1 - DEC_BATCH, D_MODEL), F32)], axis=0)
    mod = _ada_mod(cond, ada_w, ada_b)
    mod = mod.reshape(DEPTH, N_COND, 6, D_MODEL).transpose(0, 2, 1, 3).reshape(DEPTH, 6, N_COND, 1, D_MODEL)

    def ffn(x, layer, split_out):
        h = _norm_mod(x, norm2_g[layer], mod[layer, 4], mod[layer, 3])
        hidden = _mm_swiglu(h, ffn_w1, ffn_w3, layer, tm=1024, tn=256)
        half = D_FF // 2
        part = _mm_plain(hidden, ffn_w2, layer, tm=512, tn=512, kb=half, kblk=0, out_dtype=F32)
        return _mm_res([hidden], [ffn_w2], layer, [1], [1], half, x, mod[layer, 5], part, tm=512, tn=512,
                       split_out=split_out)

    h = _norm_mod(x_in, norm1_g[0], mod[0, 1], mod[0, 0])
    z = _mm_plain(h, pc_w_in, 0, tm=1024, tn=512, kb=D_MODEL, kblk=0, out_dtype=F32)
    y_pool = _pool_mix(z, pool_w[0], pool_scale[0])
    y_conv = _conv_module(z, conv_dw[0], conv_ln_g[0], conv_ln_b[0])
    x = _mm_res([y_pool, y_conv], [pc_w_out, pc_w_out], 0, [0, 0], [0, 1], POOL_WIDTH, x_in, mod[0, 2],
                tm=1024, tn=512)
    x = ffn(x, 0, False)

    h = _norm_mod(x, norm1_g[1], mod[1, 1], mod[1, 0])
    gains = jnp.stack([na_q_g[0], na_k_g[0]]).reshape(2, 1, HEAD_DIM)
    qkv, kv_prompt = _mm_qkv(h, na_w_qkv, gains, tm=1024, tn=512)
    new_k = kv_prompt[:, :D_MODEL].reshape(BATCH, 1, SEQ, N_HEADS, HEAD_DIM)
    new_v = kv_prompt[:, D_MODEL:].reshape(BATCH, 1, SEQ, N_HEADS, HEAD_DIM)
    attn_prompt = _ctx_attention(qkv)
    attn_sample = _na_attention(qkv, cache_k_na.reshape(DEC_BATCH, PAST_LEN, D_MODEL),
                                cache_v_na.reshape(DEC_BATCH, PAST_LEN, D_MODEL), _na_bias_table(na_rpb[0]))
    x = _mm_res([attn_prompt, attn_sample], [na_w_o], 0, [0], [0], D_MODEL, x, mod[1, 2], tm=1024, tn=512,
                split_x=True)
    y_prompt, y_sample = ffn(x, 1, True)

    return (y_prompt.reshape(BATCH, SEQ, D_MODEL), y_sample.reshape(DEC_BATCH, DEC_SEQ, D_MODEL), new_k, new_v)
```

```python
import functools

import numpy as np
import jax
import jax.numpy as jnp
from jax import lax
from jax.experimental import pallas as pl
from jax.experimental.pallas import tpu as pltpu

F32 = jnp.float32
BF16 = jnp.bfloat16

D_MODEL = 4096
BATCH = 16
SEQ = 256
DEPTH = 2
DEC_BATCH = 8
DEC_SEQ = 1024
PAST_LEN = 512
GRID_W = 64
HEAD_DIM = 128
N_HEADS = D_MODEL // HEAD_DIM
POOL_WIDTH = D_MODEL // 2
CONV_WIDTH = D_MODEL // 2
POOL_WINDOWS = (2, 4, 8, 16)
POOL_GROUP = POOL_WIDTH // len(POOL_WINDOWS)
CONV_K = 31
WIN_ROWS = 8
WIN_COLS = 16
D_FF = 11008
RMS_EPS = 1e-6
LN_EPS = 1e-5

M_PROMPT = BATCH * SEQ
M_SAMPLE = DEC_BATCH * DEC_SEQ
M_TOTAL = M_PROMPT + M_SAMPLE
N_COND = 16
GRID_ROWS = DEC_SEQ // GRID_W
N_LOCAL_KEYS = WIN_ROWS * GRID_W
N_REL_ROWS = 2 * WIN_ROWS - 1

V7X_VMEM_BYTES = 64 * 1024 * 1024
V7X_LANES = 128
V7X_SUBLANES = 8
COMPILER_SCRATCH_BYTES = 6 * 1024 * 1024

HALO = 16
SEQ_TILE = 256


def _params(vmem_bytes, n_axes):
    limit = min(int(vmem_bytes) + COMPILER_SCRATCH_BYTES, V7X_VMEM_BYTES - 2 * 1024 * 1024)
    return pltpu.CompilerParams(dimension_semantics=("arbitrary",) * n_axes, vmem_limit_bytes=limit)


def _cond_row(row_start):
    return jnp.maximum(row_start // DEC_SEQ - (M_PROMPT // DEC_SEQ - 1), 0)


def _split_specs(tm, tn, n_grid_axes):
    n_p = M_PROMPT // tm
    if n_grid_axes == 1:
        return (pl.BlockSpec((tm, tn), lambda i: (jnp.minimum(i, n_p - 1), 0)),
                pl.BlockSpec((tm, tn), lambda i: (jnp.maximum(i - n_p, 0), 0)))
    return (pl.BlockSpec((tm, tn), lambda j, i: (jnp.minimum(i, n_p - 1), j)),
            pl.BlockSpec((tm, tn), lambda j, i: (jnp.maximum(i - n_p, 0), j)))


def _ada_kernel(c_ref, w_ref, b_ref, o_ref):
    c = c_ref[...]
    s = (c * jax.nn.sigmoid(c)).astype(BF16)
    o_ref[...] = jnp.dot(s, w_ref[...].astype(BF16), preferred_element_type=F32) + b_ref[...]


def _ada_mod(cond, ada_w, ada_b):
    tn = 512
    n = 6 * D_MODEL
    vmem = 2 * D_MODEL * tn * 4 + D_MODEL * tn * 2 + 2 * N_COND * D_MODEL * 4
    return pl.pallas_call(
        _ada_kernel,
        out_shape=jax.ShapeDtypeStruct((DEPTH, N_COND, n), F32),
        grid=(DEPTH, n // tn),
        in_specs=[
            pl.BlockSpec((N_COND, D_MODEL), lambda l, j: (0, 0)),
            pl.BlockSpec((None, D_MODEL, tn), lambda l, j: (l, 0, j)),
            pl.BlockSpec((None, 1, tn), lambda l, j: (l, 0, j)),
        ],
        out_specs=pl.BlockSpec((None, N_COND, tn), lambda l, j: (l, 0, j)),
        compiler_params=_params(vmem, 2),
        name="ada_mod",
    )(cond, ada_w, ada_b.reshape(DEPTH, 1, n))


NORM_ROWS = 16


def _norm_mod_rows(x_ref, r_ref, gain_ref, shift_ref, o_ref):
    n_chunks = x_ref.shape[0] // NORM_ROWS
    n_tiles = x_ref.shape[1] // V7X_LANES
    for c in range(n_chunks):
        rows = slice(c * NORM_ROWS, (c + 1) * NORM_ROWS)
        x = x_ref[rows, :]
        ms = jnp.mean(x * x, axis=-1, keepdims=True)
        r_ref[rows, :] = jnp.broadcast_to(lax.rsqrt(ms + RMS_EPS), (NORM_ROWS, V7X_LANES))
    for c in range(n_chunks):
        rows = slice(c * NORM_ROWS, (c + 1) * NORM_ROWS)
        r = r_ref[rows, :]
        for t in range(n_tiles):
            lanes = slice(t * V7X_LANES, (t + 1) * V7X_LANES)
            y = x_ref[rows, lanes] * r * gain_ref[:, lanes] + shift_ref[:, lanes]
            o_ref[rows, lanes] = y.astype(o_ref.dtype)


def _norm_mod_prepare(g_ref, sc_ref, sh_ref, gain_ref, shift_ref):
    gain_ref[...] = jnp.broadcast_to(g_ref[...] * (1.0 + sc_ref[...]), gain_ref.shape)
    shift_ref[...] = jnp.broadcast_to(sh_ref[...], shift_ref.shape)


def _norm_mod_kernel(x_ref, g_ref, sc_ref, sh_ref, o_ref, r_ref, gain_ref, shift_ref):
    _norm_mod_prepare(g_ref, sc_ref, sh_ref, gain_ref, shift_ref)
    _norm_mod_rows(x_ref, r_ref, gain_ref, shift_ref, o_ref)


def _norm_mod_split_kernel(xp_ref, xs_ref, g_ref, sc_ref, sh_ref, o_ref, r_ref, gain_ref, shift_ref, *, n_p):
    _norm_mod_prepare(g_ref, sc_ref, sh_ref, gain_ref, shift_ref)

    @pl.when(pl.program_id(0) < n_p)
    def _():
        _norm_mod_rows(xp_ref, r_ref, gain_ref, shift_ref, o_ref)

    @pl.when(pl.program_id(0) >= n_p)
    def _():
        _norm_mod_rows(xs_ref, r_ref, gain_ref, shift_ref, o_ref)


def _norm_mod(x, g, scale, shift):
    tm = 256
    mod_spec = pl.BlockSpec((None, 1, D_MODEL), lambda i: (_cond_row(i * tm), 0, 0))
    if isinstance(x, tuple):
        body = functools.partial(_norm_mod_split_kernel, n_p=M_PROMPT // tm)
        x_specs = list(_split_specs(tm, D_MODEL, 1))
        xs = list(x)
    else:
        body = _norm_mod_kernel
        x_specs = [pl.BlockSpec((tm, D_MODEL), lambda i: (i, 0))]
        xs = [x]
    vmem = 2 * tm * D_MODEL * (4 * len(xs) + 2) + 4 * tm * D_MODEL * 4
    return pl.pallas_call(
        body,
        out_shape=jax.ShapeDtypeStruct((M_TOTAL, D_MODEL), BF16),
        grid=(M_TOTAL // tm,),
        in_specs=x_specs + [pl.BlockSpec((1, D_MODEL), lambda i: (0, 0)), mod_spec, mod_spec],
        out_specs=pl.BlockSpec((tm, D_MODEL), lambda i: (i, 0)),
        scratch_shapes=[
            pltpu.VMEM((tm, V7X_LANES), F32),
            pltpu.VMEM((NORM_ROWS, D_MODEL), F32),
            pltpu.VMEM((NORM_ROWS, D_MODEL), F32),
        ],
        compiler_params=_params(vmem, 1),
        name="norm_mod",
    )(*xs, g.reshape(1, D_MODEL), scale, shift)


def _cast_weights_once(w_refs, wb_refs):
    @pl.when(pl.program_id(1) == 0)
    def _():
        for w_ref, wb_ref in zip(w_refs, wb_refs):
            wb_ref[...] = w_ref[...].astype(BF16)


def _w_spec(kb, tn, layer, kblk):
    return pl.BlockSpec((None, kb, tn), lambda j, i: (layer, kblk, j))


def _mm_plain_kernel(x_ref, w_ref, o_ref, wb_ref):
    _cast_weights_once([w_ref], [wb_ref])
    o_ref[...] = jnp.dot(x_ref[...], wb_ref[...], preferred_element_type=F32).astype(o_ref.dtype)


def _mm_res_kernel(*refs, n_lhs, split_x, has_partial, n_res, n_out, n_p):
    refs = list(refs)
    x_refs = [refs.pop(0) for _ in range(2 if split_x else n_lhs)]
    w_refs = [refs.pop(0) for _ in range(n_lhs)]
    p_ref = refs.pop(0) if has_partial else None
    res_refs = [refs.pop(0) for _ in range(n_res)]
    gate_ref = refs.pop(0)
    out_refs = [refs.pop(0) for _ in range(n_out)]
    wb_refs = refs
    _cast_weights_once(w_refs, wb_refs)
    is_prompt = pl.program_id(1) < n_p

    def body(lhs_refs, side):
        acc = jnp.dot(lhs_refs[0][...], wb_refs[0][...], preferred_element_type=F32)
        for x_ref, wb_ref in zip(lhs_refs[1:], wb_refs[1:]):
            acc = acc + jnp.dot(x_ref[...], wb_ref[...], preferred_element_type=F32)
        if has_partial:
            acc = acc + p_ref[...]
        if n_res == 1:
            res = res_refs[0][...]
        elif side is None:
            res = jnp.where(is_prompt, res_refs[0][...], res_refs[1][...])
        else:
            res = res_refs[side][...]
        y = res + gate_ref[...] * acc
        if n_out == 1:
            out_refs[0][...] = y
        elif side is None:
            @pl.when(is_prompt)
            def _():
                out_refs[0][...] = y

            @pl.when(jnp.logical_not(is_prompt))
            def _():
                out_refs[1][...] = y
        else:
            out_refs[side][...] = y

    if split_x:
        @pl.when(is_prompt)
        def _():
            body([x_refs[0]], 0)

        @pl.when(jnp.logical_not(is_prompt))
        def _():
            body([x_refs[1]], 1)
    else:
        body(x_refs, None)


def _mm_swiglu_kernel(x_ref, w1_ref, w3_ref, o_ref, wb1_ref, wb3_ref):
    _cast_weights_once([w1_ref, w3_ref], [wb1_ref, wb3_ref])
    x = x_ref[...]
    a = jnp.dot(x, wb1_ref[...], preferred_element_type=F32)
    b = jnp.dot(x, wb3_ref[...], preferred_element_type=F32)
    o_ref[...] = (a * jax.nn.sigmoid(a) * b).astype(o_ref.dtype)


def _mm_qkv_kernel(x_ref, w_ref, g_ref, o_ref, k_out_ref, v_out_ref, wb_ref, acc_ref, *, tn, n_p):
    _cast_weights_once([w_ref], [wb_ref])
    j, i = pl.program_id(0), pl.program_id(1)
    tiles_per_section = D_MODEL // tn
    is_v = j >= 2 * tiles_per_section
    wants_kv = jnp.logical_and(j >= tiles_per_section, i < n_p)

    def product():
        return jnp.dot(x_ref[...], wb_ref[...], preferred_element_type=F32)

    def normed(write_kv):
        acc_ref[...] = product()
        for h in range(tn // HEAD_DIM):
            cols = slice(h * HEAD_DIM, (h + 1) * HEAD_DIM)
            z = acc_ref[:, cols]
            ms = jnp.mean(z * z, axis=-1, keepdims=True)
            zn = z * lax.rsqrt(ms + RMS_EPS) * g_ref[...]
            o_ref[:, cols] = zn.astype(o_ref.dtype)
            if write_kv:
                k_out_ref[:, cols] = zn

    def plain(write_kv):
        z = product()
        o_ref[...] = z.astype(o_ref.dtype)
        if write_kv:
            v_out_ref[...] = z

    not_v = jnp.logical_not(is_v)
    no_kv = jnp.logical_not(wants_kv)
    pl.when(jnp.logical_and(not_v, no_kv))(lambda: normed(False))
    pl.when(jnp.logical_and(not_v, wants_kv))(lambda: normed(True))
    pl.when(jnp.logical_and(is_v, no_kv))(lambda: plain(False))
    pl.when(jnp.logical_and(is_v, wants_kv))(lambda: plain(True))


def _mm_vmem(tm, tn, kb, n_lhs_bytes, n_w, out_bytes, extra_tiles):
    x = 2 * tm * n_lhs_bytes
    w = n_w * kb * tn * (2 * 4 + 2)
    o = 2 * tm * tn * out_bytes
    e = extra_tiles * 2 * tm * tn * 4
    return x + w + o + e + 2 * tm * tn * 4


def _mm_plain(x, w, layer, *, tm, tn, kb, kblk, out_dtype):
    m = x.shape[0]
    n = w.shape[2]
    vmem = _mm_vmem(tm, tn, kb, kb * 2, 1, jnp.dtype(out_dtype).itemsize, 0)
    return pl.pallas_call(
        _mm_plain_kernel,
        out_shape=jax.ShapeDtypeStruct((m, n), out_dtype),
        grid=(n // tn, m // tm),
        in_specs=[pl.BlockSpec((tm, kb), lambda j, i: (i, kblk)), _w_spec(kb, tn, layer, kblk)],
        out_specs=pl.BlockSpec((tm, tn), lambda j, i: (i, j)),
        scratch_shapes=[pltpu.VMEM((kb, tn), BF16)],
        compiler_params=_params(vmem, 2),
        name="mm_plain",
    )(x, w)


def _mm_res(xs, ws, layer, xblks, wblks, kb, res, gate, partial=None, *, tm, tn, split_x=False, split_out=False):
    n = ws[0].shape[2]
    n_lhs = len(ws)
    n_p = M_PROMPT // tm
    tile = pl.BlockSpec((tm, tn), lambda j, i: (i, j))
    if split_x:
        kk0 = xblks[0]
        in_specs = [pl.BlockSpec((tm, kb), lambda j, i: (jnp.minimum(i, n_p - 1), kk0)),
                    pl.BlockSpec((tm, kb), lambda j, i: (jnp.maximum(i - n_p, 0), kk0))]
    else:
        in_specs = [pl.BlockSpec((tm, kb), functools.partial(lambda j, i, kk: (i, kk), kk=kk)) for kk in xblks]
    in_specs += [_w_spec(kb, tn, layer, kk) for kk in wblks]
    args = list(xs) + list(ws)
    if partial is not None:
        in_specs.append(tile)
        args.append(partial)
    res_list = list(res) if isinstance(res, tuple) else [res]
    in_specs += list(_split_specs(tm, tn, 2)) if len(res_list) == 2 else [tile]
    in_specs.append(pl.BlockSpec((None, 1, tn), lambda j, i: (_cond_row(i * tm), 0, j)))
    args += res_list + [gate]
    if split_out:
        out_shape = (jax.ShapeDtypeStruct((M_PROMPT, n), F32), jax.ShapeDtypeStruct((M_SAMPLE, n), F32))
        out_specs = _split_specs(tm, tn, 2)
    else:
        out_shape = jax.ShapeDtypeStruct((M_TOTAL, n), F32)
        out_specs = tile
    n_out = 2 if split_out else 1
    vmem = _mm_vmem(tm, tn, kb, len(xs) * kb * 2, n_lhs, 4 * n_out, len(res_list) + (partial is not None))
    return pl.pallas_call(
        functools.partial(_mm_res_kernel, n_lhs=n_lhs, split_x=split_x, has_partial=partial is not None,
                          n_res=len(res_list), n_out=n_out, n_p=n_p),
        out_shape=out_shape,
        grid=(n // tn, M_TOTAL // tm),
        in_specs=in_specs,
        out_specs=out_specs,
        scratch_shapes=[pltpu.VMEM((kb, tn), BF16) for _ in range(n_lhs)],
        compiler_params=_params(vmem, 2),
        name="mm_res",
    )(*args)


def _mm_swiglu(x, w1, w3, layer, *, tm, tn):
    m, k = x.shape
    n = w1.shape[2]
    vmem = _mm_vmem(tm, tn, k, k * 2, 2, 2, 2)
    return pl.pallas_call(
        _mm_swiglu_kernel,
        out_shape=jax.ShapeDtypeStruct((m, n), BF16),
        grid=(n // tn, m // tm),
        in_specs=[pl.BlockSpec((tm, k), lambda j, i: (i, 0)), _w_spec(k, tn, layer, 0), _w_spec(k, tn, layer, 0)],
        out_specs=pl.BlockSpec((tm, tn), lambda j, i: (i, j)),
        scratch_shapes=[pltpu.VMEM((k, tn), BF16), pltpu.VMEM((k, tn), BF16)],
        compiler_params=_params(vmem, 2),
        name="mm_swiglu",
    )(x, w1, w3)


def _mm_qkv(x, w, gains, *, tm, tn):
    m, k = x.shape
    n = w.shape[2]
    tiles_per_section = D_MODEL // tn
    n_p = M_PROMPT // tm

    def k_index(j, i):
        row = jnp.where(j < tiles_per_section, 0,
                        jnp.where(j < 2 * tiles_per_section, jnp.minimum(i, n_p - 1), n_p - 1))
        return row, jnp.clip(j - tiles_per_section, 0, tiles_per_section - 1)

    def v_index(j, i):
        row = jnp.where(j < 2 * tiles_per_section, 0, jnp.minimum(i, n_p - 1))
        return row, jnp.maximum(j - 2 * tiles_per_section, 0)

    vmem = _mm_vmem(tm, tn, k, k * 2, 1, 2, 3)
    prompt_out = jax.ShapeDtypeStruct((M_PROMPT, D_MODEL), F32)
    return pl.pallas_call(
        functools.partial(_mm_qkv_kernel, tn=tn, n_p=n_p),
        out_shape=(jax.ShapeDtypeStruct((m, n), BF16), prompt_out, prompt_out),
        grid=(n // tn, m // tm),
        in_specs=[
            pl.BlockSpec((tm, k), lambda j, i: (i, 0)),
            _w_spec(k, tn, 0, 0),
            pl.BlockSpec((None, 1, HEAD_DIM), lambda j, i: (jnp.minimum(j // tiles_per_section, 1), 0, 0)),
        ],
        out_specs=(pl.BlockSpec((tm, tn), lambda j, i: (i, j)), pl.BlockSpec((tm, tn), k_index),
                   pl.BlockSpec((tm, tn), v_index)),
        scratch_shapes=[pltpu.VMEM((k, tn), BF16), pltpu.VMEM((tm, tn), F32)],
        compiler_params=_params(vmem, 2),
        name="mm_qkv",
    )(x, w, gains)


def _seq_tile_position(i):
    is_prompt = i < M_PROMPT // SEQ_TILE
    tiles_per_seq = DEC_SEQ // SEQ_TILE
    off = jnp.where(is_prompt, 0, (jnp.maximum(i - M_PROMPT // SEQ_TILE, 0) % tiles_per_seq) * SEQ_TILE)
    length = jnp.where(is_prompt, SEQ, DEC_SEQ)
    return off, length


def _halo_specs(width, col_block):
    n_halo_blocks = M_TOTAL // HALO
    per_tile = SEQ_TILE // HALO
    main = pl.BlockSpec((SEQ_TILE, width), lambda i: (i, col_block))
    prev = pl.BlockSpec((HALO, width), lambda i: (jnp.maximum(i * per_tile - 1, 0), col_block))
    nxt = pl.BlockSpec((HALO, width), lambda i: (jnp.minimum((i + 1) * per_tile, n_halo_blocks - 1), col_block))
    return main, prev, nxt


def _pool_kernel(zm_ref, zp_ref, zn_ref, w_ref, s_ref, o_ref, buf_ref, wb_ref):
    i = pl.program_id(0)

    @pl.when(i == 0)
    def _():
        wb_ref[...] = w_ref[...].astype(BF16)

    off, length = _seq_tile_position(i)
    buf_ref[0:HALO, :] = jnp.where(off == 0, 0.0, zp_ref[...])
    buf_ref[HALO:HALO + SEQ_TILE, :] = zm_ref[...]
    buf_ref[HALO + SEQ_TILE:, :] = jnp.where(off + SEQ_TILE == length, 0.0, zn_ref[...])

    rows = 64
    for g, win in enumerate(POOL_WINDOWS):
        lo, hi = win // 2, win - win // 2 - 1
        cols = slice(g * POOL_GROUP, (g + 1) * POOL_GROUP)
        w_g = wb_ref[g]
        for rc in range(SEQ_TILE // rows):
            r0 = rc * rows
            acc = buf_ref[HALO + r0 - lo:HALO + r0 - lo + rows, cols]
            for s in range(-lo + 1, hi + 1):
                acc = acc + buf_ref[HALO + r0 + s:HALO + r0 + s + rows, cols]
            t = off + r0 + lax.broadcasted_iota(jnp.int32, (rows, 1), 0)
            count = (jnp.minimum(t + hi + 1, length) - jnp.maximum(t - lo, 0)).astype(F32)
            mixed = (acc / count - zm_ref[r0:r0 + rows, cols]).astype(BF16)
            y = jnp.dot(mixed, w_g, preferred_element_type=F32) * s_ref[:, cols]
            o_ref[r0:r0 + rows, cols] = y.astype(o_ref.dtype)


def _pool_mix(z, pool_w, pool_scale):
    main, prev, nxt = _halo_specs(POOL_WIDTH, 0)
    n_g = len(POOL_WINDOWS)
    vmem = (2 * (SEQ_TILE + 2 * HALO) * POOL_WIDTH * 4 + (SEQ_TILE + 2 * HALO) * POOL_WIDTH * 4
            + n_g * POOL_GROUP * POOL_GROUP * (2 * 4 + 2) + 2 * SEQ_TILE * POOL_WIDTH * 2)
    return pl.pallas_call(
        _pool_kernel,
        out_shape=jax.ShapeDtypeStruct((M_TOTAL, POOL_WIDTH), BF16),
        grid=(M_TOTAL // SEQ_TILE,),
        in_specs=[
            main, prev, nxt,
            pl.BlockSpec((n_g, POOL_GROUP, POOL_GROUP), lambda i: (0, 0, 0)),
            pl.BlockSpec((1, POOL_WIDTH), lambda i: (0, 0)),
        ],
        out_specs=pl.BlockSpec((SEQ_TILE, POOL_WIDTH), lambda i: (i, 0)),
        scratch_shapes=[
            pltpu.VMEM((SEQ_TILE + 2 * HALO, POOL_WIDTH), F32),
            pltpu.VMEM((n_g, POOL_GROUP, POOL_GROUP), BF16),
        ],
        compiler_params=_params(vmem, 1),
        name="pool_mix",
    )(z, z, z, pool_w, pool_scale.reshape(1, POOL_WIDTH))


def _conv_kernel(am_ref, ap_ref, an_ref, gm_ref, gp_ref, gn_ref, w_ref, lg_ref, lb_ref, o_ref, buf_ref, y_ref,
                 win_ref):
    i = pl.program_id(0)
    off, length = _seq_tile_position(i)

    def glu(a_ref, g_ref):
        return a_ref[...] * jax.nn.sigmoid(g_ref[...])

    buf_ref[0:HALO, :] = jnp.where(off == 0, 0.0, glu(ap_ref, gp_ref))
    buf_ref[HALO:HALO + SEQ_TILE, :] = glu(am_ref, gm_ref)
    buf_ref[HALO + SEQ_TILE:, :] = jnp.where(off + SEQ_TILE == length, 0.0, glu(an_ref, gn_ref))

    rows = 128
    first_tap_row = HALO - CONV_K // 2

    def channel_tile(c, carry):
        cols = pl.ds(pl.multiple_of(c * V7X_LANES, V7X_LANES), V7X_LANES)
        taps = [w_ref[k:k + 1, cols] for k in range(CONV_K)]
        for rc in range(SEQ_TILE // rows):
            acc = None
            for phase in range(V7X_SUBLANES):
                ks = [k for k in range(CONV_K) if (first_tap_row + k) % V7X_SUBLANES == phase]
                base = first_tap_row + ks[0] + rc * rows
                n_win = rows + V7X_SUBLANES * (len(ks) - 1)
                win_ref[phase, 0:n_win, :] = buf_ref[base:base + n_win, cols]
                for a, k in enumerate(ks):
                    term = win_ref[phase, V7X_SUBLANES * a:V7X_SUBLANES * a + rows, :] * taps[k]
                    acc = term if acc is None else acc + term
            y_ref[rc * rows:(rc + 1) * rows, cols] = acc
        return carry

    lax.fori_loop(0, CONV_WIDTH // V7X_LANES, channel_tile, 0)

    ln_rows = 64
    for rc in range(SEQ_TILE // ln_rows):
        rs = slice(rc * ln_rows, (rc + 1) * ln_rows)
        y = y_ref[rs, :]
        mu = jnp.mean(y, axis=-1, keepdims=True)
        d = y - mu
        var = jnp.mean(d * d, axis=-1, keepdims=True)
        yn = d * lax.rsqrt(var + LN_EPS) * lg_ref[...] + lb_ref[...]
        o_ref[rs, :] = (yn * jax.nn.sigmoid(yn)).astype(o_ref.dtype)


def _conv_module(z, conv_dw, ln_g, ln_b):
    am, ap, an = _halo_specs(CONV_WIDTH, 1)
    gm, gp, gn = _halo_specs(CONV_WIDTH, 2)
    vec = pl.BlockSpec((1, CONV_WIDTH), lambda i: (0, 0))
    tile_rows = SEQ_TILE + 2 * HALO
    vmem = (2 * 2 * tile_rows * CONV_WIDTH * 4 + tile_rows * CONV_WIDTH * 4 + SEQ_TILE * CONV_WIDTH * 4
            + 2 * SEQ_TILE * CONV_WIDTH * 2 + 4 * SEQ_TILE * CONV_WIDTH * 4)
    return pl.pallas_call(
        _conv_kernel,
        out_shape=jax.ShapeDtypeStruct((M_TOTAL, CONV_WIDTH), BF16),
        grid=(M_TOTAL // SEQ_TILE,),
        in_specs=[am, ap, an, gm, gp, gn, pl.BlockSpec((CONV_K, CONV_WIDTH), lambda i: (0, 0)), vec, vec],
        out_specs=pl.BlockSpec((SEQ_TILE, CONV_WIDTH), lambda i: (i, 0)),
        scratch_shapes=[
            pltpu.VMEM((tile_rows, CONV_WIDTH), F32),
            pltpu.VMEM((SEQ_TILE, CONV_WIDTH), F32),
            pltpu.VMEM((V7X_SUBLANES, 128 + V7X_SUBLANES * (-(-CONV_K // V7X_SUBLANES) - 1), V7X_LANES), F32),
        ],
        compiler_params=_params(vmem, 1),
        name="conv_module",
    )(z, z, z, z, z, z, conv_dw, ln_g.reshape(1, CONV_WIDTH), ln_b.reshape(1, CONV_WIDTH))


_NT_DIMS = (((1,), (1,)), ((), ()))
LOG2_E = float(np.log2(np.e))
Q_SCALE = HEAD_DIM ** -0.5 * LOG2_E
SOFTMAX_ROWS = 16


def _softmax_numerators(s_ref, e_ref, l_ref, n_rows, col0, n_cols, row0=0):
    l_lanes = slice(col0 // n_cols * V7X_LANES, (col0 // n_cols + 1) * V7X_LANES)
    chunks = [slice(row0 + c * SOFTMAX_ROWS, row0 + (c + 1) * SOFTMAX_ROWS) for c in range(n_rows // SOFTMAX_ROWS)]
    for rows in chunks:
        m = jnp.max(s_ref[rows, col0:col0 + n_cols], axis=-1, keepdims=True)
        l_ref[rows, l_lanes] = jnp.broadcast_to(m, (SOFTMAX_ROWS, V7X_LANES))
    for rows in chunks:
        m = l_ref[rows, l_lanes]
        acc = None
        for t in range(n_cols // V7X_LANES):
            lanes = slice(col0 + t * V7X_LANES, col0 + (t + 1) * V7X_LANES)
            e = jnp.exp2(s_ref[rows, lanes] - m)
            e_ref[rows, lanes] = e.astype(BF16)
            acc = e if acc is None else acc + e
        l_ref[rows, l_lanes] = jnp.broadcast_to(jnp.sum(acc, axis=-1, keepdims=True), (SOFTMAX_ROWS, V7X_LANES))


def _ctx_attn_kernel(q_ref, k_ref, v_ref, o_ref, s_ref, e_ref, l_ref, *, heads):
    for h in range(heads):
        cols = slice(h * HEAD_DIM, (h + 1) * HEAD_DIM)
        s_ref[:, h * SEQ:(h + 1) * SEQ] = lax.dot_general(
            q_ref[:, cols], k_ref[:, cols], _NT_DIMS, preferred_element_type=F32)
    for h in range(heads):
        _softmax_numerators(s_ref, e_ref, l_ref, SEQ, h * SEQ, SEQ)
    for h in range(heads):
        cols = slice(h * HEAD_DIM, (h + 1) * HEAD_DIM)
        o = jnp.dot(e_ref[:, h * SEQ:(h + 1) * SEQ], v_ref[:, cols], preferred_element_type=F32) / l_ref[:, cols]
        o_ref[:, cols] = o.astype(o_ref.dtype)


def _ctx_attention(qkv):
    heads = 8
    width = heads * HEAD_DIM
    sections = D_MODEL // width

    def spec(section):
        return pl.BlockSpec((SEQ, width), lambda b, g: (b, section * sections + g))

    vmem = 2 * 4 * SEQ * width * 2 + SEQ * heads * SEQ * 6 + SEQ * width * 4 + 4 * SEQ * heads * SEQ * 4
    return pl.pallas_call(
        functools.partial(_ctx_attn_kernel, heads=heads),
        out_shape=jax.ShapeDtypeStruct((M_PROMPT, D_MODEL), BF16),
        grid=(BATCH, sections),
        in_specs=[spec(0), spec(1), spec(2)],
        out_specs=pl.BlockSpec((SEQ, width), lambda b, g: (b, g)),
        scratch_shapes=[
            pltpu.VMEM((SEQ, heads * SEQ), F32),
            pltpu.VMEM((SEQ, heads * SEQ), BF16),
            pltpu.VMEM((SEQ, width), F32),
        ],
        compiler_params=_params(vmem, 2),
        name="ctx_attention",
    )(qkv, qkv, qkv)


def _window_start(r):
    return min(max(r - WIN_ROWS // 2, 0), GRID_ROWS - WIN_ROWS)


def _na_attn_kernel(q_ref, k_ref, v_ref, kc_ref, vc_ref, t2_ref, o_ref, s_ref, e_ref, l_ref, oc_ref, *, heads):
    n_loc = N_LOCAL_KEYS
    n_keys = n_loc + PAST_LEN

    def head_cols(h):
        return slice(h * HEAD_DIM, (h + 1) * HEAD_DIM)

    def row_slices(r):
        ws = _window_start(r)
        return slice(r * GRID_W, (r + 1) * GRID_W), slice(ws * GRID_W, ws * GRID_W + n_loc), ws

    def ctx_scores(h):
        cols = head_cols(h)
        s_ref[h % 2, :, n_loc:] = lax.dot_general(q_ref[:, cols], kc_ref[h].astype(BF16), _NT_DIMS,
                                                  preferred_element_type=F32)

    def local_scores(h, r):
        cols = head_cols(h)
        rows, keys, ws = row_slices(r)
        s = lax.dot_general(q_ref[rows, cols], k_ref[keys, cols], _NT_DIMS, preferred_element_type=F32)
        d0 = ws - r + WIN_ROWS - 1
        for m in range(n_loc // V7X_LANES):
            lanes = slice(m * V7X_LANES, (m + 1) * V7X_LANES)
            s_ref[h % 2, rows, lanes] = s[:, lanes] + t2_ref[h, d0 + 2 * m]

    def softmax_row(h, r):
        p = h % 2
        _softmax_numerators(s_ref.at[p], e_ref.at[p], l_ref.at[p], GRID_W, 0, n_keys, row0=r * GRID_W)

    def ctx_values(h):
        oc_ref[...] = jnp.dot(e_ref[h % 2, :, n_loc:], vc_ref[h].astype(BF16),
                              preferred_element_type=F32)

    def local_values(h, r):
        cols = head_cols(h)
        rows, keys, _ = row_slices(r)
        o = jnp.dot(e_ref[h % 2, rows, :n_loc], v_ref[keys, cols], preferred_element_type=F32) + oc_ref[rows, :]
        o_ref[rows, cols] = (o / l_ref[h % 2, rows, :]).astype(o_ref.dtype)

    for step in range(heads + 2):
        h_scores, h_softmax, h_values = step, step - 1, step - 2
        if 0 <= h_values < heads:
            ctx_values(h_values)
        if h_scores < heads:
            ctx_scores(h_scores)
        for r in range(GRID_ROWS):
            if 0 <= h_softmax < heads:
                softmax_row(h_softmax, r)
            if h_scores < heads:
                local_scores(h_scores, r)
            if 0 <= h_values < heads:
                local_values(h_values, r)


def _na_attention(qkv, cache_k, cache_v, t2):
    heads = 4
    width = heads * HEAD_DIM
    sections = D_MODEL // width
    row0 = M_PROMPT // DEC_SEQ

    def spec(section):
        return pl.BlockSpec((DEC_SEQ, width), lambda b, g: (row0 + b, section * sections + g))

    cache_spec = pl.BlockSpec((None, heads, PAST_LEN, HEAD_DIM), lambda b, g: (b, g, 0, 0))
    n_keys = N_LOCAL_KEYS + PAST_LEN
    n_t2 = N_REL_ROWS - 1
    vmem = (2 * 4 * DEC_SEQ * width * 2 + 2 * 2 * PAST_LEN * width * 4
            + 2 * heads * n_t2 * GRID_W * V7X_LANES * 4
            + 2 * DEC_SEQ * n_keys * 6 + 3 * DEC_SEQ * V7X_LANES * 4 + 6 * DEC_SEQ * HEAD_DIM * 4)
    return pl.pallas_call(
        functools.partial(_na_attn_kernel, heads=heads),
        out_shape=jax.ShapeDtypeStruct((M_SAMPLE, D_MODEL), BF16),
        grid=(DEC_BATCH, sections),
        in_specs=[
            spec(0), spec(1), spec(2), cache_spec, cache_spec,
            pl.BlockSpec((heads, n_t2, GRID_W, V7X_LANES), lambda b, g: (g, 0, 0, 0)),
        ],
        out_specs=pl.BlockSpec((DEC_SEQ, width), lambda b, g: (b, g)),
        scratch_shapes=[
            pltpu.VMEM((2, DEC_SEQ, n_keys), F32),
            pltpu.VMEM((2, DEC_SEQ, n_keys), BF16),
            pltpu.VMEM((2, DEC_SEQ, V7X_LANES), F32),
            pltpu.VMEM((DEC_SEQ, HEAD_DIM), F32),
        ],
        compiler_params=_params(vmem, 2),
        name="na_attention",
    )(qkv, qkv, qkv, cache_k, cache_v, t2)


def _na_bias_table(rpb):
    n_dc = 2 * WIN_COLS - 1
    qc = np.arange(GRID_W)[:, None]
    kc = np.arange(GRID_W)[None, :]
    cstart = np.clip(qc - WIN_COLS // 2, 0, GRID_W - WIN_COLS)
    col_ok = (kc >= cstart) & (kc < cstart + WIN_COLS)
    dc = np.clip(kc - qc, -(WIN_COLS - 1), WIN_COLS - 1) + WIN_COLS - 1
    one_hot = (dc[None] == np.arange(n_dc)[:, None, None]).astype(np.float32)
    select = np.zeros((2 * n_dc, GRID_W, 2 * GRID_W), np.float32)
    select[:n_dc, :, :GRID_W] = one_hot
    select[n_dc:, :, GRID_W:] = one_hot
    mask = np.where(np.concatenate([col_ok, col_ok], axis=1), 0.0, -np.inf).astype(np.float32)
    pairs = jnp.concatenate([rpb[:, :-1], rpb[:, 1:]], axis=-1) * LOG2_E
    t2 = jnp.einsum("hdk,kqc->hdqc", pairs, jnp.asarray(select), precision=lax.Precision.HIGHEST)
    return t2 + jnp.asarray(mask)


def kernel(x_prompt, x_sample, cache_k_na, cache_v_na, c, c_ctx, ada_w, ada_b, norm1_g, norm2_g, ffn_w1, ffn_w3,
           ffn_w2, pc_w_in, pool_w, pool_scale, conv_dw, conv_ln_g, conv_ln_b, pc_w_out, na_w_qkv, na_q_g, na_k_g,
           na_rpb, na_w_o):
    x_in = (x_prompt.reshape(M_PROMPT, D_MODEL), x_sample.reshape(M_SAMPLE, D_MODEL))
    cond = jnp.concatenate([c_ctx[None], c, jnp.zeros((N_COND - 1 - DEC_BATCH, D_MODEL), F32)], axis=0)
    mod = _ada_mod(cond, ada_w, ada_b)
    mod = mod.reshape(DEPTH, N_COND, 6, D_MODEL).transpose(0, 2, 1, 3).reshape(DEPTH, 6, N_COND, 1, D_MODEL)

    def ffn(x, layer, split_out):
        h = _norm_mod(x, norm2_g[layer], mod[layer, 4], mod[layer, 3])
        hidden = _mm_swiglu(h, ffn_w1, ffn_w3, layer, tm=1024, tn=256)
        half = D_FF // 2
        part = _mm_plain(hidden, ffn_w2, layer, tm=512, tn=512, kb=half, kblk=0, out_dtype=F32)
        return _mm_res([hidden], [ffn_w2], layer, [1], [1], half, x, mod[layer, 5], part, tm=512, tn=512,
                       split_out=split_out)

    h = _norm_mod(x_in, norm1_g[0], mod[0, 1], mod[0, 0])
    z = _mm_plain(h, pc_w_in, 0, tm=1024, tn=512, kb=D_MODEL, kblk=0, out_dtype=F32)
    y_pool = _pool_mix(z, pool_w[0], pool_scale[0])
    y_conv = _conv_module(z, conv_dw[0], conv_ln_g[0], conv_ln_b[0])
    x = _mm_res([y_pool, y_conv], [pc_w_out, pc_w_out], 0, [0, 0], [0, 1], POOL_WIDTH, x_in, mod[0, 2],
                tm=1024, tn=512)
    x = ffn(x, 0, False)

    h = _norm_mod(x, norm1_g[1], mod[1, 1], mod[1, 0])
    gains = jnp.stack([na_q_g[0] * Q_SCALE, na_k_g[0]]).reshape(2, 1, HEAD_DIM)
    qkv, k_prompt, v_prompt = _mm_qkv(h, na_w_qkv, gains, tm=1024, tn=512)
    new_k = k_prompt.reshape(BATCH, 1, SEQ, N_HEADS, HEAD_DIM)
    new_v = v_prompt.reshape(BATCH, 1, SEQ, N_HEADS, HEAD_DIM)
    attn_prompt = _ctx_attention(qkv)
    cache_k = cache_k_na.reshape(DEC_BATCH, PAST_LEN, N_HEADS, HEAD_DIM).transpose(0, 2, 1, 3)
    cache_v = cache_v_na.reshape(DEC_BATCH, PAST_LEN, N_HEADS, HEAD_DIM).transpose(0, 2, 1, 3)
    attn_sample = _na_attention(qkv, cache_k, cache_v, _na_bias_table(na_rpb[0]))
    x = _mm_res([attn_prompt, attn_sample], [na_w_o], 0, [0], [0], D_MODEL, x, mod[1, 2], tm=1024, tn=512,
                split_x=True)
    y_prompt, y_sample = ffn(x, 1, True)

    return (y_prompt.reshape(BATCH, SEQ, D_MODEL), y_sample.reshape(DEC_BATCH, DEC_SEQ, D_MODEL), new_k, new_v)
```

```python
import functools

import numpy as np
import jax
import jax.numpy as jnp
from jax import lax
from jax.experimental import pallas as pl
from jax.experimental.pallas import tpu as pltpu

F32 = jnp.float32
BF16 = jnp.bfloat16

D_MODEL = 4096
BATCH = 16
SEQ = 256
DEPTH = 2
DEC_BATCH = 8
DEC_SEQ = 1024
PAST_LEN = 512
GRID_W = 64
HEAD_DIM = 128
N_HEADS = D_MODEL // HEAD_DIM
POOL_WIDTH = D_MODEL // 2
CONV_WIDTH = D_MODEL // 2
POOL_WINDOWS = (2, 4, 8, 16)
POOL_GROUP = POOL_WIDTH // len(POOL_WINDOWS)
CONV_K = 31
WIN_ROWS = 8
WIN_COLS = 16
D_FF = 11008
RMS_EPS = 1e-6
LN_EPS = 1e-5

M_PROMPT = BATCH * SEQ
M_SAMPLE = DEC_BATCH * DEC_SEQ
M_TOTAL = M_PROMPT + M_SAMPLE
N_COND = 16
GRID_ROWS = DEC_SEQ // GRID_W
N_LOCAL_KEYS = WIN_ROWS * GRID_W
N_REL_ROWS = 2 * WIN_ROWS - 1

V7X_VMEM_BYTES = 64 * 1024 * 1024
V7X_LANES = 128
V7X_SUBLANES = 8
COMPILER_SCRATCH_BYTES = 6 * 1024 * 1024

HALO = 16
SEQ_TILE = 256


def _params(vmem_bytes, n_axes):
    limit = min(int(vmem_bytes) + COMPILER_SCRATCH_BYTES, V7X_VMEM_BYTES - 2 * 1024 * 1024)
    return pltpu.CompilerParams(dimension_semantics=("arbitrary",) * n_axes, vmem_limit_bytes=limit)


def _cond_row(row_start):
    return jnp.maximum(row_start // DEC_SEQ - (M_PROMPT // DEC_SEQ - 1), 0)


def _split_specs(tm, tn, n_grid_axes):
    n_p = M_PROMPT // tm
    if n_grid_axes == 1:
        return (pl.BlockSpec((tm, tn), lambda i: (jnp.minimum(i, n_p - 1), 0)),
                pl.BlockSpec((tm, tn), lambda i: (jnp.maximum(i - n_p, 0), 0)))
    return (pl.BlockSpec((tm, tn), lambda j, i: (jnp.minimum(i, n_p - 1), j)),
            pl.BlockSpec((tm, tn), lambda j, i: (jnp.maximum(i - n_p, 0), j)))


def _ada_kernel(c_ref, w_ref, b_ref, o_ref):
    c = c_ref[...]
    s = (c * jax.nn.sigmoid(c)).astype(BF16)
    o_ref[...] = jnp.dot(s, w_ref[...].astype(BF16), preferred_element_type=F32) + b_ref[...]


def _ada_mod(cond, ada_w, ada_b):
    tn = 512
    n = 6 * D_MODEL
    vmem = 2 * D_MODEL * tn * 4 + D_MODEL * tn * 2 + 2 * N_COND * D_MODEL * 4
    return pl.pallas_call(
        _ada_kernel,
        out_shape=jax.ShapeDtypeStruct((DEPTH, N_COND, n), F32),
        grid=(DEPTH, n // tn),
        in_specs=[
            pl.BlockSpec((N_COND, D_MODEL), lambda l, j: (0, 0)),
            pl.BlockSpec((None, D_MODEL, tn), lambda l, j: (l, 0, j)),
            pl.BlockSpec((None, 1, tn), lambda l, j: (l, 0, j)),
        ],
        out_specs=pl.BlockSpec((None, N_COND, tn), lambda l, j: (l, 0, j)),
        compiler_params=_params(vmem, 2),
        name="ada_mod",
    )(cond, ada_w, ada_b.reshape(DEPTH, 1, n))


NORM_ROWS = 16


def _norm_mod_rows(x_ref, r_ref, gain_ref, shift_ref, o_ref):
    n_chunks = x_ref.shape[0] // NORM_ROWS
    n_tiles = x_ref.shape[1] // V7X_LANES
    for c in range(n_chunks):
        rows = slice(c * NORM_ROWS, (c + 1) * NORM_ROWS)
        x = x_ref[rows, :]
        ms = jnp.mean(x * x, axis=-1, keepdims=True)
        r_ref[rows, :] = jnp.broadcast_to(lax.rsqrt(ms + RMS_EPS), (NORM_ROWS, V7X_LANES))
    for c in range(n_chunks):
        rows = slice(c * NORM_ROWS, (c + 1) * NORM_ROWS)
        r = r_ref[rows, :]
        for t in range(n_tiles):
            lanes = slice(t * V7X_LANES, (t + 1) * V7X_LANES)
            y = x_ref[rows, lanes] * r * gain_ref[:, lanes] + shift_ref[:, lanes]
            o_ref[rows, lanes] = y.astype(o_ref.dtype)


def _norm_mod_prepare(g_ref, sc_ref, sh_ref, gain_ref, shift_ref):
    gain_ref[...] = jnp.broadcast_to(g_ref[...] * (1.0 + sc_ref[...]), gain_ref.shape)
    shift_ref[...] = jnp.broadcast_to(sh_ref[...], shift_ref.shape)


def _norm_mod_kernel(x_ref, g_ref, sc_ref, sh_ref, o_ref, r_ref, gain_ref, shift_ref):
    _norm_mod_prepare(g_ref, sc_ref, sh_ref, gain_ref, shift_ref)
    _norm_mod_rows(x_ref, r_ref, gain_ref, shift_ref, o_ref)


def _norm_mod_split_kernel(xp_ref, xs_ref, g_ref, sc_ref, sh_ref, o_ref, r_ref, gain_ref, shift_ref, *, n_p):
    _norm_mod_prepare(g_ref, sc_ref, sh_ref, gain_ref, shift_ref)

    @pl.when(pl.program_id(0) < n_p)
    def _():
        _norm_mod_rows(xp_ref, r_ref, gain_ref, shift_ref, o_ref)

    @pl.when(pl.program_id(0) >= n_p)
    def _():
        _norm_mod_rows(xs_ref, r_ref, gain_ref, shift_ref, o_ref)


def _norm_mod(x, g, scale, shift):
    tm = 256
    mod_spec = pl.BlockSpec((None, 1, D_MODEL), lambda i: (_cond_row(i * tm), 0, 0))
    if isinstance(x, tuple):
        body = functools.partial(_norm_mod_split_kernel, n_p=M_PROMPT // tm)
        x_specs = list(_split_specs(tm, D_MODEL, 1))
        xs = list(x)
    else:
        body = _norm_mod_kernel
        x_specs = [pl.BlockSpec((tm, D_MODEL), lambda i: (i, 0))]
        xs = [x]
    vmem = 2 * tm * D_MODEL * (4 * len(xs) + 2) + 4 * tm * D_MODEL * 4
    return pl.pallas_call(
        body,
        out_shape=jax.ShapeDtypeStruct((M_TOTAL, D_MODEL), BF16),
        grid=(M_TOTAL // tm,),
        in_specs=x_specs + [pl.BlockSpec((1, D_MODEL), lambda i: (0, 0)), mod_spec, mod_spec],
        out_specs=pl.BlockSpec((tm, D_MODEL), lambda i: (i, 0)),
        scratch_shapes=[
            pltpu.VMEM((tm, V7X_LANES), F32),
            pltpu.VMEM((NORM_ROWS, D_MODEL), F32),
            pltpu.VMEM((NORM_ROWS, D_MODEL), F32),
        ],
        compiler_params=_params(vmem, 1),
        name="norm_mod",
    )(*xs, g.reshape(1, D_MODEL), scale, shift)


def _cast_weights_once(w_refs, wb_refs):
    @pl.when(pl.program_id(1) == 0)
    def _():
        for w_ref, wb_ref in zip(w_refs, wb_refs):
            wb_ref[...] = w_ref[...].astype(BF16)


def _w_spec(kb, tn, layer, kblk):
    return pl.BlockSpec((None, kb, tn), lambda j, i: (layer, kblk, j))


def _mm_plain_kernel(x_ref, w_ref, o_ref, wb_ref):
    _cast_weights_once([w_ref], [wb_ref])
    o_ref[...] = jnp.dot(x_ref[...], wb_ref[...], preferred_element_type=F32).astype(o_ref.dtype)


def _mm_res_kernel(*refs, n_lhs, split_x, has_partial, n_res, n_out, n_p):
    refs = list(refs)
    x_refs = [refs.pop(0) for _ in range(2 if split_x else n_lhs)]
    w_refs = [refs.pop(0) for _ in range(n_lhs)]
    p_ref = refs.pop(0) if has_partial else None
    res_refs = [refs.pop(0) for _ in range(n_res)]
    gate_ref = refs.pop(0)
    out_refs = [refs.pop(0) for _ in range(n_out)]
    wb_refs = refs
    _cast_weights_once(w_refs, wb_refs)
    is_prompt = pl.program_id(1) < n_p

    def body(lhs_refs, side):
        acc = jnp.dot(lhs_refs[0][...], wb_refs[0][...], preferred_element_type=F32)
        for x_ref, wb_ref in zip(lhs_refs[1:], wb_refs[1:]):
            acc = acc + jnp.dot(x_ref[...], wb_ref[...], preferred_element_type=F32)
        if has_partial:
            acc = acc + p_ref[...]
        if n_res == 1:
            res = res_refs[0][...]
        elif side is None:
            res = jnp.where(is_prompt, res_refs[0][...], res_refs[1][...])
        else:
            res = res_refs[side][...]
        y = res + gate_ref[...] * acc
        if n_out == 1:
            out_refs[0][...] = y
        elif side is None:
            @pl.when(is_prompt)
            def _():
                out_refs[0][...] = y

            @pl.when(jnp.logical_not(is_prompt))
            def _():
                out_refs[1][...] = y
        else:
            out_refs[side][...] = y

    if split_x:
        @pl.when(is_prompt)
        def _():
            body([x_refs[0]], 0)

        @pl.when(jnp.logical_not(is_prompt))
        def _():
            body([x_refs[1]], 1)
    else:
        body(x_refs, None)


def _mm_swiglu_kernel(x_ref, w1_ref, w3_ref, o_ref, wb1_ref, wb3_ref):
    _cast_weights_once([w1_ref, w3_ref], [wb1_ref, wb3_ref])
    x = x_ref[...]
    a = jnp.dot(x, wb1_ref[...], preferred_element_type=F32)
    b = jnp.dot(x, wb3_ref[...], preferred_element_type=F32)
    o_ref[...] = (a * jax.nn.sigmoid(a) * b).astype(o_ref.dtype)


def _mm_qkv_kernel(x_ref, w_ref, g_ref, o_ref, k_out_ref, v_out_ref, wb_ref, acc_ref, *, tn, n_p):
    _cast_weights_once([w_ref], [wb_ref])
    j, i = pl.program_id(0), pl.program_id(1)
    tiles_per_section = D_MODEL // tn
    is_v = j >= 2 * tiles_per_section
    wants_kv = jnp.logical_and(j >= tiles_per_section, i < n_p)

    def product():
        return jnp.dot(x_ref[...], wb_ref[...], preferred_element_type=F32)

    def normed(write_kv):
        acc_ref[...] = product()
        for h in range(tn // HEAD_DIM):
            cols = slice(h * HEAD_DIM, (h + 1) * HEAD_DIM)
            z = acc_ref[:, cols]
            ms = jnp.mean(z * z, axis=-1, keepdims=True)
            zn = z * lax.rsqrt(ms + RMS_EPS) * g_ref[...]
            o_ref[:, cols] = zn.astype(o_ref.dtype)
            if write_kv:
                k_out_ref[:, cols] = zn

    def plain(write_kv):
        z = product()
        o_ref[...] = z.astype(o_ref.dtype)
        if write_kv:
            v_out_ref[...] = z

    not_v = jnp.logical_not(is_v)
    no_kv = jnp.logical_not(wants_kv)
    pl.when(jnp.logical_and(not_v, no_kv))(lambda: normed(False))
    pl.when(jnp.logical_and(not_v, wants_kv))(lambda: normed(True))
    pl.when(jnp.logical_and(is_v, no_kv))(lambda: plain(False))
    pl.when(jnp.logical_and(is_v, wants_kv))(lambda: plain(True))


def _mm_vmem(tm, tn, kb, n_lhs_bytes, n_w, out_bytes, extra_tiles):
    x = 2 * tm * n_lhs_bytes
    w = n_w * kb * tn * (2 * 4 + 2)
    o = 2 * tm * tn * out_bytes
    e = extra_tiles * 2 * tm * tn * 4
    return x + w + o + e + 2 * tm * tn * 4


def _mm_plain(x, w, layer, *, tm, tn, kb, kblk, out_dtype):
    m = x.shape[0]
    n = w.shape[2]
    vmem = _mm_vmem(tm, tn, kb, kb * 2, 1, jnp.dtype(out_dtype).itemsize, 0)
    return pl.pallas_call(
        _mm_plain_kernel,
        out_shape=jax.ShapeDtypeStruct((m, n), out_dtype),
        grid=(n // tn, m // tm),
        in_specs=[pl.BlockSpec((tm, kb), lambda j, i: (i, kblk)), _w_spec(kb, tn, layer, kblk)],
        out_specs=pl.BlockSpec((tm, tn), lambda j, i: (i, j)),
        scratch_shapes=[pltpu.VMEM((kb, tn), BF16)],
        compiler_params=_params(vmem, 2),
        name="mm_plain",
    )(x, w)


def _mm_res(xs, ws, layer, xblks, wblks, kb, res, gate, partial=None, *, tm, tn, split_x=False, split_out=False):
    n = ws[0].shape[2]
    n_lhs = len(ws)
    n_p = M_PROMPT // tm
    tile = pl.BlockSpec((tm, tn), lambda j, i: (i, j))
    if split_x:
        kk0 = xblks[0]
        in_specs = [pl.BlockSpec((tm, kb), lambda j, i: (jnp.minimum(i, n_p - 1), kk0)),
                    pl.BlockSpec((tm, kb), lambda j, i: (jnp.maximum(i - n_p, 0), kk0))]
    else:
        in_specs = [pl.BlockSpec((tm, kb), functools.partial(lambda j, i, kk: (i, kk), kk=kk)) for kk in xblks]
    in_specs += [_w_spec(kb, tn, layer, kk) for kk in wblks]
    args = list(xs) + list(ws)
    if partial is not None:
        in_specs.append(tile)
        args.append(partial)
    res_list = list(res) if isinstance(res, tuple) else [res]
    in_specs += list(_split_specs(tm, tn, 2)) if len(res_list) == 2 else [tile]
    in_specs.append(pl.BlockSpec((None, 1, tn), lambda j, i: (_cond_row(i * tm), 0, j)))
    args += res_list + [gate]
    if split_out:
        out_shape = (jax.ShapeDtypeStruct((M_PROMPT, n), F32), jax.ShapeDtypeStruct((M_SAMPLE, n), F32))
        out_specs = _split_specs(tm, tn, 2)
    else:
        out_shape = jax.ShapeDtypeStruct((M_TOTAL, n), F32)
        out_specs = tile
    n_out = 2 if split_out else 1
    vmem = _mm_vmem(tm, tn, kb, len(xs) * kb * 2, n_lhs, 4 * n_out, len(res_list) + (partial is not None))
    return pl.pallas_call(
        functools.partial(_mm_res_kernel, n_lhs=n_lhs, split_x=split_x, has_partial=partial is not None,
                          n_res=len(res_list), n_out=n_out, n_p=n_p),
        out_shape=out_shape,
        grid=(n // tn, M_TOTAL // tm),
        in_specs=in_specs,
        out_specs=out_specs,
        scratch_shapes=[pltpu.VMEM((kb, tn), BF16) for _ in range(n_lhs)],
        compiler_params=_params(vmem, 2),
        name="mm_res",
    )(*args)


def _mm_swiglu(x, w1, w3, layer, *, tm, tn):
    m, k = x.shape
    n = w1.shape[2]
    vmem = _mm_vmem(tm, tn, k, k * 2, 2, 2, 2)
    return pl.pallas_call(
        _mm_swiglu_kernel,
        out_shape=jax.ShapeDtypeStruct((m, n), BF16),
        grid=(n // tn, m // tm),
        in_specs=[pl.BlockSpec((tm, k), lambda j, i: (i, 0)), _w_spec(k, tn, layer, 0), _w_spec(k, tn, layer, 0)],
        out_specs=pl.BlockSpec((tm, tn), lambda j, i: (i, j)),
        scratch_shapes=[pltpu.VMEM((k, tn), BF16), pltpu.VMEM((k, tn), BF16)],
        compiler_params=_params(vmem, 2),
        name="mm_swiglu",
    )(x, w1, w3)


def _mm_qkv(x, w, gains, *, tm, tn):
    m, k = x.shape
    n = w.shape[2]
    tiles_per_section = D_MODEL // tn
    n_p = M_PROMPT // tm

    def k_index(j, i):
        row = jnp.where(j < tiles_per_section, 0,
                        jnp.where(j < 2 * tiles_per_section, jnp.minimum(i, n_p - 1), n_p - 1))
        return row, jnp.clip(j - tiles_per_section, 0, tiles_per_section - 1)

    def v_index(j, i):
        row = jnp.where(j < 2 * tiles_per_section, 0, jnp.minimum(i, n_p - 1))
        return row, jnp.maximum(j - 2 * tiles_per_section, 0)

    vmem = _mm_vmem(tm, tn, k, k * 2, 1, 2, 3)
    prompt_out = jax.ShapeDtypeStruct((M_PROMPT, D_MODEL), F32)
    return pl.pallas_call(
        functools.partial(_mm_qkv_kernel, tn=tn, n_p=n_p),
        out_shape=(jax.ShapeDtypeStruct((m, n), BF16), prompt_out, prompt_out),
        grid=(n // tn, m // tm),
        in_specs=[
            pl.BlockSpec((tm, k), lambda j, i: (i, 0)),
            _w_spec(k, tn, 0, 0),
            pl.BlockSpec((None, 1, HEAD_DIM), lambda j, i: (jnp.minimum(j // tiles_per_section, 1), 0, 0)),
        ],
        out_specs=(pl.BlockSpec((tm, tn), lambda j, i: (i, j)), pl.BlockSpec((tm, tn), k_index),
                   pl.BlockSpec((tm, tn), v_index)),
        scratch_shapes=[pltpu.VMEM((k, tn), BF16), pltpu.VMEM((tm, tn), F32)],
        compiler_params=_params(vmem, 2),
        name="mm_qkv",
    )(x, w, gains)


def _seq_tile_position(i):
    is_prompt = i < M_PROMPT // SEQ_TILE
    tiles_per_seq = DEC_SEQ // SEQ_TILE
    off = jnp.where(is_prompt, 0, (jnp.maximum(i - M_PROMPT // SEQ_TILE, 0) % tiles_per_seq) * SEQ_TILE)
    length = jnp.where(is_prompt, SEQ, DEC_SEQ)
    return off, length


def _halo_specs(width, col_block):
    n_halo_blocks = M_TOTAL // HALO
    per_tile = SEQ_TILE // HALO
    main = pl.BlockSpec((SEQ_TILE, width), lambda i: (i, col_block))
    prev = pl.BlockSpec((HALO, width), lambda i: (jnp.maximum(i * per_tile - 1, 0), col_block))
    nxt = pl.BlockSpec((HALO, width), lambda i: (jnp.minimum((i + 1) * per_tile, n_halo_blocks - 1), col_block))
    return main, prev, nxt


POOL_BUF_ROWS = SEQ_TILE + 3 * HALO


def _pool_kernel(zm_ref, zp_ref, zn_ref, w_ref, s_ref, o_ref, buf_ref, sa_ref, sb_ref, wb_ref):
    i = pl.program_id(0)

    @pl.when(i == 0)
    def _():
        wb_ref[...] = w_ref[...].astype(BF16)

    off, length = _seq_tile_position(i)
    buf_ref[0:HALO, :] = jnp.where(off == 0, 0.0, zp_ref[...])
    buf_ref[HALO:HALO + SEQ_TILE, :] = zm_ref[...]
    buf_ref[HALO + SEQ_TILE:2 * HALO + SEQ_TILE, :] = jnp.where(off + SEQ_TILE == length, 0.0, zn_ref[...])
    buf_ref[2 * HALO + SEQ_TILE:, :] = jnp.zeros((HALO, POOL_WIDTH), F32)

    first = V7X_SUBLANES
    top_end = HALO + SEQ_TILE
    rows = 64
    for g, win in enumerate(POOL_WINDOWS):
        lo, hi = win // 2, win - win // 2 - 1
        cols = slice(g * POOL_GROUP, (g + 1) * POOL_GROUP)
        w_g = wb_ref[g]
        n_levels = min(win.bit_length() - 1, 3)
        ends = [top_end + V7X_SUBLANES * (n_levels - 1 - k) for k in range(n_levels)]
        sa_ref[first:ends[0], :] = buf_ref[first:ends[0], cols] + buf_ref[first + 1:ends[0] + 1, cols]
        top_ref = sa_ref
        if n_levels >= 2:
            sb_ref[first:ends[1], :] = sa_ref[first:ends[1], :] + sa_ref[first + 2:ends[1] + 2, :]
            top_ref = sb_ref
        if n_levels >= 3:
            sa_ref[first:ends[2], :] = sb_ref[first:ends[2], :] + sb_ref[first + 4:ends[2] + 4, :]
            top_ref = sa_ref
        for rc in range(SEQ_TILE // rows):
            r0 = rc * rows
            b0 = HALO + r0 - lo
            acc = top_ref[b0:b0 + rows, :]
            if win > 2 ** n_levels:
                acc = acc + top_ref[b0 + 2 ** n_levels:b0 + 2 ** n_levels + rows, :]
            t = off + r0 + lax.broadcasted_iota(jnp.int32, (rows, 1), 0)
            count = (jnp.minimum(t + hi + 1, length) - jnp.maximum(t - lo, 0)).astype(F32)
            mixed = (acc / count - zm_ref[r0:r0 + rows, cols]).astype(BF16)
            y = jnp.dot(mixed, w_g, preferred_element_type=F32) * s_ref[:, cols]
            o_ref[r0:r0 + rows, cols] = y.astype(o_ref.dtype)


def _pool_mix(z, pool_w, pool_scale):
    main, prev, nxt = _halo_specs(POOL_WIDTH, 0)
    n_g = len(POOL_WINDOWS)
    vmem = (2 * (SEQ_TILE + 2 * HALO) * POOL_WIDTH * 4 + POOL_BUF_ROWS * (POOL_WIDTH + 2 * POOL_GROUP) * 4
            + n_g * POOL_GROUP * POOL_GROUP * (2 * 4 + 2) + 2 * SEQ_TILE * POOL_WIDTH * 2)
    return pl.pallas_call(
        _pool_kernel,
        out_shape=jax.ShapeDtypeStruct((M_TOTAL, POOL_WIDTH), BF16),
        grid=(M_TOTAL // SEQ_TILE,),
        in_specs=[
            main, prev, nxt,
            pl.BlockSpec((n_g, POOL_GROUP, POOL_GROUP), lambda i: (0, 0, 0)),
            pl.BlockSpec((1, POOL_WIDTH), lambda i: (0, 0)),
        ],
        out_specs=pl.BlockSpec((SEQ_TILE, POOL_WIDTH), lambda i: (i, 0)),
        scratch_shapes=[
            pltpu.VMEM((POOL_BUF_ROWS, POOL_WIDTH), F32),
            pltpu.VMEM((POOL_BUF_ROWS, POOL_GROUP), F32),
            pltpu.VMEM((POOL_BUF_ROWS, POOL_GROUP), F32),
            pltpu.VMEM((n_g, POOL_GROUP, POOL_GROUP), BF16),
        ],
        compiler_params=_params(vmem, 1),
        name="pool_mix",
    )(z, z, z, pool_w, pool_scale.reshape(1, POOL_WIDTH))


def _conv_kernel(am_ref, ap_ref, an_ref, gm_ref, gp_ref, gn_ref, w_ref, lg_ref, lb_ref, o_ref, buf_ref, y_ref,
                 win_ref):
    i = pl.program_id(0)
    off, length = _seq_tile_position(i)

    def glu(a_ref, g_ref):
        return a_ref[...] * jax.nn.sigmoid(g_ref[...])

    buf_ref[0:HALO, :] = jnp.where(off == 0, 0.0, glu(ap_ref, gp_ref))
    buf_ref[HALO:HALO + SEQ_TILE, :] = glu(am_ref, gm_ref)
    buf_ref[HALO + SEQ_TILE:, :] = jnp.where(off + SEQ_TILE == length, 0.0, glu(an_ref, gn_ref))

    rows = 128
    first_tap_row = HALO - CONV_K // 2

    def channel_tile(c, carry):
        cols = pl.ds(pl.multiple_of(c * V7X_LANES, V7X_LANES), V7X_LANES)
        taps = [w_ref[k:k + 1, cols] for k in range(CONV_K)]
        for rc in range(SEQ_TILE // rows):
            acc = None
            for phase in range(V7X_SUBLANES):
                ks = [k for k in range(CONV_K) if (first_tap_row + k) % V7X_SUBLANES == phase]
                base = first_tap_row + ks[0] + rc * rows
                n_win = rows + V7X_SUBLANES * (len(ks) - 1)
                win_ref[phase, 0:n_win, :] = buf_ref[base:base + n_win, cols]
                for a, k in enumerate(ks):
                    term = win_ref[phase, V7X_SUBLANES * a:V7X_SUBLANES * a + rows, :] * taps[k]
                    acc = term if acc is None else acc + term
            y_ref[rc * rows:(rc + 1) * rows, cols] = acc
        return carry

    lax.fori_loop(0, CONV_WIDTH // V7X_LANES, channel_tile, 0)

    ln_rows = 64
    for rc in range(SEQ_TILE // ln_rows):
        rs = slice(rc * ln_rows, (rc + 1) * ln_rows)
        y = y_ref[rs, :]
        mu = jnp.mean(y, axis=-1, keepdims=True)
        d = y - mu
        var = jnp.mean(d * d, axis=-1, keepdims=True)
        yn = d * lax.rsqrt(var + LN_EPS) * lg_ref[...] + lb_ref[...]
        o_ref[rs, :] = (yn * jax.nn.sigmoid(yn)).astype(o_ref.dtype)


def _conv_module(z, conv_dw, ln_g, ln_b):
    am, ap, an = _halo_specs(CONV_WIDTH, 1)
    gm, gp, gn = _halo_specs(CONV_WIDTH, 2)
    vec = pl.BlockSpec((1, CONV_WIDTH), lambda i: (0, 0))
    tile_rows = SEQ_TILE + 2 * HALO
    vmem = (2 * 2 * tile_rows * CONV_WIDTH * 4 + tile_rows * CONV_WIDTH * 4 + SEQ_TILE * CONV_WIDTH * 4
            + 2 * SEQ_TILE * CONV_WIDTH * 2 + 4 * SEQ_TILE * CONV_WIDTH * 4)
    return pl.pallas_call(
        _conv_kernel,
        out_shape=jax.ShapeDtypeStruct((M_TOTAL, CONV_WIDTH), BF16),
        grid=(M_TOTAL // SEQ_TILE,),
        in_specs=[am, ap, an, gm, gp, gn, pl.BlockSpec((CONV_K, CONV_WIDTH), lambda i: (0, 0)), vec, vec],
        out_specs=pl.BlockSpec((SEQ_TILE, CONV_WIDTH), lambda i: (i, 0)),
        scratch_shapes=[
            pltpu.VMEM((tile_rows, CONV_WIDTH), F32),
            pltpu.VMEM((SEQ_TILE, CONV_WIDTH), F32),
            pltpu.VMEM((V7X_SUBLANES, 128 + V7X_SUBLANES * (-(-CONV_K // V7X_SUBLANES) - 1), V7X_LANES), F32),
        ],
        compiler_params=_params(vmem, 1),
        name="conv_module",
    )(z, z, z, z, z, z, conv_dw, ln_g.reshape(1, CONV_WIDTH), ln_b.reshape(1, CONV_WIDTH))


_NT_DIMS = (((1,), (1,)), ((), ()))
LOG2_E = float(np.log2(np.e))
Q_SCALE = HEAD_DIM ** -0.5 * LOG2_E
SOFTMAX_ROWS = 16


def _softmax_numerators(s_ref, e_ref, l_ref, n_rows, col0, n_cols, row0=0):
    l_lanes = slice(col0 // n_cols * V7X_LANES, (col0 // n_cols + 1) * V7X_LANES)
    chunks = [slice(row0 + c * SOFTMAX_ROWS, row0 + (c + 1) * SOFTMAX_ROWS) for c in range(n_rows // SOFTMAX_ROWS)]
    for rows in chunks:
        m = jnp.max(s_ref[rows, col0:col0 + n_cols], axis=-1, keepdims=True)
        l_ref[rows, l_lanes] = jnp.broadcast_to(m, (SOFTMAX_ROWS, V7X_LANES))
    for rows in chunks:
        m = l_ref[rows, l_lanes]
        acc = None
        for t in range(n_cols // V7X_LANES):
            lanes = slice(col0 + t * V7X_LANES, col0 + (t + 1) * V7X_LANES)
            e = jnp.exp2(s_ref[rows, lanes] - m)
            e_ref[rows, lanes] = e.astype(BF16)
            acc = e if acc is None else acc + e
        l_ref[rows, l_lanes] = jnp.broadcast_to(jnp.sum(acc, axis=-1, keepdims=True), (SOFTMAX_ROWS, V7X_LANES))


def _ctx_attn_kernel(q_ref, k_ref, v_ref, o_ref, s_ref, e_ref, l_ref, *, heads):
    for h in range(heads):
        cols = slice(h * HEAD_DIM, (h + 1) * HEAD_DIM)
        s_ref[:, h * SEQ:(h + 1) * SEQ] = lax.dot_general(
            q_ref[:, cols], k_ref[:, cols], _NT_DIMS, preferred_element_type=F32)
    for h in range(heads):
        _softmax_numerators(s_ref, e_ref, l_ref, SEQ, h * SEQ, SEQ)
    for h in range(heads):
        cols = slice(h * HEAD_DIM, (h + 1) * HEAD_DIM)
        o = jnp.dot(e_ref[:, h * SEQ:(h + 1) * SEQ], v_ref[:, cols], preferred_element_type=F32) / l_ref[:, cols]
        o_ref[:, cols] = o.astype(o_ref.dtype)


def _ctx_attention(qkv):
    heads = 8
    width = heads * HEAD_DIM
    sections = D_MODEL // width

    def spec(section):
        return pl.BlockSpec((SEQ, width), lambda b, g: (b, section * sections + g))

    vmem = 2 * 4 * SEQ * width * 2 + SEQ * heads * SEQ * 6 + SEQ * width * 4 + 4 * SEQ * heads * SEQ * 4
    return pl.pallas_call(
        functools.partial(_ctx_attn_kernel, heads=heads),
        out_shape=jax.ShapeDtypeStruct((M_PROMPT, D_MODEL), BF16),
        grid=(BATCH, sections),
        in_specs=[spec(0), spec(1), spec(2)],
        out_specs=pl.BlockSpec((SEQ, width), lambda b, g: (b, g)),
        scratch_shapes=[
            pltpu.VMEM((SEQ, heads * SEQ), F32),
            pltpu.VMEM((SEQ, heads * SEQ), BF16),
            pltpu.VMEM((SEQ, width), F32),
        ],
        compiler_params=_params(vmem, 2),
        name="ctx_attention",
    )(qkv, qkv, qkv)


def _window_start(r):
    return min(max(r - WIN_ROWS // 2, 0), GRID_ROWS - WIN_ROWS)


def _na_attn_kernel(q_ref, k_ref, v_ref, kc_ref, vc_ref, t2_ref, o_ref, s_ref, e_ref, l_ref, oc_ref, *, heads):
    n_loc = N_LOCAL_KEYS
    n_keys = n_loc + PAST_LEN

    def head_cols(h):
        return slice(h * HEAD_DIM, (h + 1) * HEAD_DIM)

    def row_slices(r):
        ws = _window_start(r)
        return slice(r * GRID_W, (r + 1) * GRID_W), slice(ws * GRID_W, ws * GRID_W + n_loc), ws

    def ctx_scores(h):
        cols = head_cols(h)
        s_ref[h % 2, :, n_loc:] = lax.dot_general(q_ref[:, cols], kc_ref[h].astype(BF16), _NT_DIMS,
                                                  preferred_element_type=F32)

    def local_scores(h, r):
        cols = head_cols(h)
        rows, keys, ws = row_slices(r)
        s = lax.dot_general(q_ref[rows, cols], k_ref[keys, cols], _NT_DIMS, preferred_element_type=F32)
        d0 = ws - r + WIN_ROWS - 1
        for m in range(n_loc // V7X_LANES):
            lanes = slice(m * V7X_LANES, (m + 1) * V7X_LANES)
            s_ref[h % 2, rows, lanes] = s[:, lanes] + t2_ref[h, d0 + 2 * m]

    def softmax_row(h, r):
        p = h % 2
        _softmax_numerators(s_ref.at[p], e_ref.at[p], l_ref.at[p], GRID_W, 0, n_keys, row0=r * GRID_W)

    def ctx_values(h):
        oc_ref[...] = jnp.dot(e_ref[h % 2, :, n_loc:], vc_ref[h].astype(BF16),
                              preferred_element_type=F32)

    def local_values(h, r):
        cols = head_cols(h)
        rows, keys, _ = row_slices(r)
        o = jnp.dot(e_ref[h % 2, rows, :n_loc], v_ref[keys, cols], preferred_element_type=F32) + oc_ref[rows, :]
        o_ref[rows, cols] = (o / l_ref[h % 2, rows, :]).astype(o_ref.dtype)

    for step in range(heads + 2):
        h_scores, h_softmax, h_values = step, step - 1, step - 2
        if 0 <= h_values < heads:
            ctx_values(h_values)
        if h_scores < heads:
            ctx_scores(h_scores)
        for r in range(GRID_ROWS):
            if 0 <= h_softmax < heads:
                softmax_row(h_softmax, r)
            if h_scores < heads:
                local_scores(h_scores, r)
            if 0 <= h_values < heads:
                local_values(h_values, r)


def _na_attention(qkv, cache_k, cache_v, t2):
    heads = 4
    width = heads * HEAD_DIM
    sections = D_MODEL // width
    row0 = M_PROMPT // DEC_SEQ

    def spec(section):
        return pl.BlockSpec((DEC_SEQ, width), lambda b, g: (row0 + b, section * sections + g))

    cache_spec = pl.BlockSpec((None, heads, PAST_LEN, HEAD_DIM), lambda b, g: (b, g, 0, 0))
    n_keys = N_LOCAL_KEYS + PAST_LEN
    n_t2 = N_REL_ROWS - 1
    vmem = (2 * 4 * DEC_SEQ * width * 2 + 2 * 2 * PAST_LEN * width * 4
            + 2 * heads * n_t2 * GRID_W * V7X_LANES * 4
            + 2 * DEC_SEQ * n_keys * 6 + 3 * DEC_SEQ * V7X_LANES * 4 + 6 * DEC_SEQ * HEAD_DIM * 4)
    return pl.pallas_call(
        functools.partial(_na_attn_kernel, heads=heads),
        out_shape=jax.ShapeDtypeStruct((M_SAMPLE, D_MODEL), BF16),
        grid=(DEC_BATCH, sections),
        in_specs=[
            spec(0), spec(1), spec(2), cache_spec, cache_spec,
            pl.BlockSpec((heads, n_t2, GRID_W, V7X_LANES), lambda b, g: (g, 0, 0, 0)),
        ],
        out_specs=pl.BlockSpec((DEC_SEQ, width), lambda b, g: (b, g)),
        scratch_shapes=[
            pltpu.VMEM((2, DEC_SEQ, n_keys), F32),
            pltpu.VMEM((2, DEC_SEQ, n_keys), BF16),
            pltpu.VMEM((2, DEC_SEQ, V7X_LANES), F32),
            pltpu.VMEM((DEC_SEQ, HEAD_DIM), F32),
        ],
        compiler_params=_params(vmem, 2),
        name="na_attention",
    )(qkv, qkv, qkv, cache_k, cache_v, t2)


def _na_bias_table(rpb):
    n_dc = 2 * WIN_COLS - 1
    qc = np.arange(GRID_W)[:, None]
    kc = np.arange(GRID_W)[None, :]
    cstart = np.clip(qc - WIN_COLS // 2, 0, GRID_W - WIN_COLS)
    col_ok = (kc >= cstart) & (kc < cstart + WIN_COLS)
    dc = np.clip(kc - qc, -(WIN_COLS - 1), WIN_COLS - 1) + WIN_COLS - 1
    one_hot = (dc[None] == np.arange(n_dc)[:, None, None]).astype(np.float32)
    select = np.zeros((2 * n_dc, GRID_W, 2 * GRID_W), np.float32)
    select[:n_dc, :, :GRID_W] = one_hot
    select[n_dc:, :, GRID_W:] = one_hot
    mask = np.where(np.concatenate([col_ok, col_ok], axis=1), 0.0, -np.inf).astype(np.float32)
    pairs = jnp.concatenate([rpb[:, :-1], rpb[:, 1:]], axis=-1) * LOG2_E
    t2 = jnp.einsum("hdk,kqc->hdqc", pairs, jnp.asarray(select), precision=lax.Precision.HIGHEST)
    return t2 + jnp.asarray(mask)


def kernel(x_prompt, x_sample, cache_k_na, cache_v_na, c, c_ctx, ada_w, ada_b, norm1_g, norm2_g, ffn_w1, ffn_w3,
           ffn_w2, pc_w_in, pool_w, pool_scale, conv_dw, conv_ln_g, conv_ln_b, pc_w_out, na_w_qkv, na_q_g, na_k_g,
           na_rpb, na_w_o):
    x_in = (x_prompt.reshape(M_PROMPT, D_MODEL), x_sample.reshape(M_SAMPLE, D_MODEL))
    cond = jnp.concatenate([c_ctx[None], c, jnp.zeros((N_COND - 1 - DEC_BATCH, D_MODEL), F32)], axis=0)
    mod = _ada_mod(cond, ada_w, ada_b)
    mod = mod.reshape(DEPTH, N_COND, 6, D_MODEL).transpose(0, 2, 1, 3).reshape(DEPTH, 6, N_COND, 1, D_MODEL)

    def ffn(x, layer, split_out):
        h = _norm_mod(x, norm2_g[layer], mod[layer, 4], mod[layer, 3])
        hidden = _mm_swiglu(h, ffn_w1, ffn_w3, layer, tm=2048, tn=256)
        half = D_FF // 2
        part = _mm_plain(hidden, ffn_w2, layer, tm=1024, tn=512, kb=half, kblk=0, out_dtype=F32)
        return _mm_res([hidden], [ffn_w2], layer, [1], [1], half, x, mod[layer, 5], part, tm=512, tn=512,
                       split_out=split_out)

    h = _norm_mod(x_in, norm1_g[0], mod[0, 1], mod[0, 0])
    z = _mm_plain(h, pc_w_in, 0, tm=1024, tn=512, kb=D_MODEL, kblk=0, out_dtype=F32)
    y_pool = _pool_mix(z, pool_w[0], pool_scale[0])
    y_conv = _conv_module(z, conv_dw[0], conv_ln_g[0], conv_ln_b[0])
    x = _mm_res([y_pool, y_conv], [pc_w_out, pc_w_out], 0, [0, 0], [0, 1], POOL_WIDTH, x_in, mod[0, 2],
                tm=1024, tn=512)
    x = ffn(x, 0, False)

    h = _norm_mod(x, norm1_g[1], mod[1, 1], mod[1, 0])
    gains = jnp.stack([na_q_g[0] * Q_SCALE, na_k_g[0]]).reshape(2, 1, HEAD_DIM)
    qkv, k_prompt, v_prompt = _mm_qkv(h, na_w_qkv, gains, tm=1024, tn=512)
    new_k = k_prompt.reshape(BATCH, 1, SEQ, N_HEADS, HEAD_DIM)
    new_v = v_prompt.reshape(BATCH, 1, SEQ, N_HEADS, HEAD_DIM)
    cache_k = cache_k_na.reshape(DEC_BATCH, PAST_LEN, N_HEADS, HEAD_DIM).transpose(0, 2, 1, 3)
    cache_v = cache_v_na.reshape(DEC_BATCH, PAST_LEN, N_HEADS, HEAD_DIM).transpose(0, 2, 1, 3)
    attn_sample = _na_attention(qkv, cache_k, cache_v, _na_bias_table(na_rpb[0]))
    attn_prompt = _ctx_attention(qkv)
    x = _mm_res([attn_prompt, attn_sample], [na_w_o], 0, [0], [0], D_MODEL, x, mod[1, 2], tm=1024, tn=512,
                split_x=True)
    y_prompt, y_sample = ffn(x, 1, True)

    return (y_prompt.reshape(BATCH, SEQ, D_MODEL), y_sample.reshape(DEC_BATCH, DEC_SEQ, D_MODEL), new_k, new_v)
```

```python
import functools

import numpy as np
import jax
import jax.numpy as jnp
from jax import lax
from jax.experimental import pallas as pl
from jax.experimental.pallas import tpu as pltpu

F32 = jnp.float32
BF16 = jnp.bfloat16

D_MODEL = 4096
BATCH = 16
SEQ = 256
DEPTH = 2
DEC_BATCH = 8
DEC_SEQ = 1024
PAST_LEN = 512
GRID_W = 64
HEAD_DIM = 128
N_HEADS = D_MODEL // HEAD_DIM
POOL_WIDTH = D_MODEL // 2
CONV_WIDTH = D_MODEL // 2
POOL_WINDOWS = (2, 4, 8, 16)
POOL_GROUP = POOL_WIDTH // len(POOL_WINDOWS)
CONV_K = 31
WIN_ROWS = 8
WIN_COLS = 16
D_FF = 11008
RMS_EPS = 1e-6
LN_EPS = 1e-5

M_PROMPT = BATCH * SEQ
M_SAMPLE = DEC_BATCH * DEC_SEQ
M_TOTAL = M_PROMPT + M_SAMPLE
N_COND = 16
GRID_ROWS = DEC_SEQ // GRID_W
N_LOCAL_KEYS = WIN_ROWS * GRID_W
N_REL_ROWS = 2 * WIN_ROWS - 1

V7X_VMEM_BYTES = 64 * 1024 * 1024
V7X_LANES = 128
V7X_SUBLANES = 8
COMPILER_SCRATCH_BYTES = 6 * 1024 * 1024

HALO = 16
SEQ_TILE = 256


def _params(vmem_bytes, n_axes):
    limit = min(int(vmem_bytes) + COMPILER_SCRATCH_BYTES, V7X_VMEM_BYTES - 2 * 1024 * 1024)
    return pltpu.CompilerParams(dimension_semantics=("arbitrary",) * n_axes, vmem_limit_bytes=limit)


def _cond_row(row_start):
    return jnp.maximum(row_start // DEC_SEQ - (M_PROMPT // DEC_SEQ - 1), 0)


def _split_specs(tm, tn, n_grid_axes):
    n_p = M_PROMPT // tm
    if n_grid_axes == 1:
        return (pl.BlockSpec((tm, tn), lambda i: (jnp.minimum(i, n_p - 1), 0)),
                pl.BlockSpec((tm, tn), lambda i: (jnp.maximum(i - n_p, 0), 0)))
    return (pl.BlockSpec((tm, tn), lambda j, i: (jnp.minimum(i, n_p - 1), j)),
            pl.BlockSpec((tm, tn), lambda j, i: (jnp.maximum(i - n_p, 0), j)))


def _ada_kernel(c_ref, w_ref, b_ref, o_ref):
    c = c_ref[...]
    s = (c * jax.nn.sigmoid(c)).astype(BF16)
    o_ref[...] = jnp.dot(s, w_ref[...].astype(BF16), preferred_element_type=F32) + b_ref[...]


def _ada_mod(cond, ada_w, ada_b):
    tn = 1024
    n = 6 * D_MODEL
    vmem = 2 * D_MODEL * tn * 4 + D_MODEL * tn * 2 + 2 * N_COND * D_MODEL * 4
    return pl.pallas_call(
        _ada_kernel,
        out_shape=jax.ShapeDtypeStruct((DEPTH, N_COND, n), F32),
        grid=(DEPTH, n // tn),
        in_specs=[
            pl.BlockSpec((N_COND, D_MODEL), lambda l, j: (0, 0)),
            pl.BlockSpec((None, D_MODEL, tn), lambda l, j: (l, 0, j)),
            pl.BlockSpec((None, 1, tn), lambda l, j: (l, 0, j)),
        ],
        out_specs=pl.BlockSpec((None, N_COND, tn), lambda l, j: (l, 0, j)),
        compiler_params=_params(vmem, 2),
        name="ada_mod",
    )(cond, ada_w, ada_b.reshape(DEPTH, 1, n))


NORM_ROWS = 16


def _norm_mod_rows(x_ref, r_ref, gain_ref, shift_ref, o_ref):
    n_chunks = x_ref.shape[0] // NORM_ROWS
    n_tiles = x_ref.shape[1] // V7X_LANES
    for c in range(n_chunks):
        rows = slice(c * NORM_ROWS, (c + 1) * NORM_ROWS)
        x = x_ref[rows, :]
        ms = jnp.mean(x * x, axis=-1, keepdims=True)
        r_ref[rows, :] = jnp.broadcast_to(lax.rsqrt(ms + RMS_EPS), (NORM_ROWS, V7X_LANES))
    for c in range(n_chunks):
        rows = slice(c * NORM_ROWS, (c + 1) * NORM_ROWS)
        r = r_ref[rows, :]
        for t in range(n_tiles):
            lanes = slice(t * V7X_LANES, (t + 1) * V7X_LANES)
            y = x_ref[rows, lanes] * r * gain_ref[:, lanes] + shift_ref[:, lanes]
            o_ref[rows, lanes] = y.astype(o_ref.dtype)


def _norm_mod_prepare(g_ref, sc_ref, sh_ref, gain_ref, shift_ref):
    gain_ref[...] = jnp.broadcast_to(g_ref[...] * (1.0 + sc_ref[...]), gain_ref.shape)
    shift_ref[...] = jnp.broadcast_to(sh_ref[...], shift_ref.shape)


def _norm_mod_kernel(x_ref, g_ref, sc_ref, sh_ref, o_ref, r_ref, gain_ref, shift_ref):
    _norm_mod_prepare(g_ref, sc_ref, sh_ref, gain_ref, shift_ref)
    _norm_mod_rows(x_ref, r_ref, gain_ref, shift_ref, o_ref)


def _norm_mod_split_kernel(xp_ref, xs_ref, g_ref, sc_ref, sh_ref, o_ref, r_ref, gain_ref, shift_ref, *, n_p):
    _norm_mod_prepare(g_ref, sc_ref, sh_ref, gain_ref, shift_ref)

    @pl.when(pl.program_id(0) < n_p)
    def _():
        _norm_mod_rows(xp_ref, r_ref, gain_ref, shift_ref, o_ref)

    @pl.when(pl.program_id(0) >= n_p)
    def _():
        _norm_mod_rows(xs_ref, r_ref, gain_ref, shift_ref, o_ref)


def _norm_mod(x, g, scale, shift):
    tm = 512
    mod_spec = pl.BlockSpec((None, 1, D_MODEL), lambda i: (_cond_row(i * tm), 0, 0))
    if isinstance(x, tuple):
        body = functools.partial(_norm_mod_split_kernel, n_p=M_PROMPT // tm)
        x_specs = list(_split_specs(tm, D_MODEL, 1))
        xs = list(x)
    else:
        body = _norm_mod_kernel
        x_specs = [pl.BlockSpec((tm, D_MODEL), lambda i: (i, 0))]
        xs = [x]
    vmem = 2 * tm * D_MODEL * (4 * len(xs) + 2) + 4 * tm * D_MODEL * 4
    return pl.pallas_call(
        body,
        out_shape=jax.ShapeDtypeStruct((M_TOTAL, D_MODEL), BF16),
        grid=(M_TOTAL // tm,),
        in_specs=x_specs + [pl.BlockSpec((1, D_MODEL), lambda i: (0, 0)), mod_spec, mod_spec],
        out_specs=pl.BlockSpec((tm, D_MODEL), lambda i: (i, 0)),
        scratch_shapes=[
            pltpu.VMEM((tm, V7X_LANES), F32),
            pltpu.VMEM((NORM_ROWS, D_MODEL), F32),
            pltpu.VMEM((NORM_ROWS, D_MODEL), F32),
        ],
        compiler_params=_params(vmem, 1),
        name="norm_mod",
    )(*xs, g.reshape(1, D_MODEL), scale, shift)


def _cast_weights_once(w_refs, wb_refs):
    @pl.when(pl.program_id(1) == 0)
    def _():
        for w_ref, wb_ref in zip(w_refs, wb_refs):
            wb_ref[...] = w_ref[...].astype(BF16)


def _w_spec(kb, tn, layer, kblk):
    return pl.BlockSpec((None, kb, tn), lambda j, i: (layer, kblk, j))


def _mm_plain_kernel(x_ref, w_ref, o_ref, wb_ref):
    _cast_weights_once([w_ref], [wb_ref])
    o_ref[...] = jnp.dot(x_ref[...], wb_ref[...], preferred_element_type=F32).astype(o_ref.dtype)


def _mm_res_kernel(*refs, n_lhs, split_x, has_partial, n_res, n_out, n_p):
    refs = list(refs)
    x_refs = [refs.pop(0) for _ in range(2 if split_x else n_lhs)]
    w_refs = [refs.pop(0) for _ in range(n_lhs)]
    p_ref = refs.pop(0) if has_partial else None
    res_refs = [refs.pop(0) for _ in range(n_res)]
    gate_ref = refs.pop(0)
    out_refs = [refs.pop(0) for _ in range(n_out)]
    wb_refs = refs
    _cast_weights_once(w_refs, wb_refs)
    is_prompt = pl.program_id(1) < n_p

    def body(lhs_refs, side):
        acc = jnp.dot(lhs_refs[0][...], wb_refs[0][...], preferred_element_type=F32)
        for x_ref, wb_ref in zip(lhs_refs[1:], wb_refs[1:]):
            acc = acc + jnp.dot(x_ref[...], wb_ref[...], preferred_element_type=F32)
        if has_partial:
            acc = acc + p_ref[...]
        if n_res == 1:
            res = res_refs[0][...]
        elif side is None:
            res = jnp.where(is_prompt, res_refs[0][...], res_refs[1][...])
        else:
            res = res_refs[side][...]
        y = res + gate_ref[...] * acc
        if n_out == 1:
            out_refs[0][...] = y
        elif side is None:
            @pl.when(is_prompt)
            def _():
                out_refs[0][...] = y

            @pl.when(jnp.logical_not(is_prompt))
            def _():
                out_refs[1][...] = y
        else:
            out_refs[side][...] = y

    if split_x:
        @pl.when(is_prompt)
        def _():
            body([x_refs[0]], 0)

        @pl.when(jnp.logical_not(is_prompt))
        def _():
            body([x_refs[1]], 1)
    else:
        body(x_refs, None)


def _mm_swiglu_kernel(x_ref, w1_ref, w3_ref, o_ref, wb1_ref, wb3_ref):
    _cast_weights_once([w1_ref, w3_ref], [wb1_ref, wb3_ref])
    x = x_ref[...]
    a = jnp.dot(x, wb1_ref[...], preferred_element_type=F32)
    b = jnp.dot(x, wb3_ref[...], preferred_element_type=F32)
    o_ref[...] = (a * jax.nn.sigmoid(a) * b).astype(o_ref.dtype)


def _mm_qkv_kernel(x_ref, w_ref, g_ref, o_ref, k_out_ref, v_out_ref, wb_ref, acc_ref, *, tn, n_p):
    _cast_weights_once([w_ref], [wb_ref])
    j, i = pl.program_id(0), pl.program_id(1)
    tiles_per_section = D_MODEL // tn
    is_v = j >= 2 * tiles_per_section
    wants_kv = jnp.logical_and(j >= tiles_per_section, i < n_p)

    def product():
        return jnp.dot(x_ref[...], wb_ref[...], preferred_element_type=F32)

    def normed(write_kv):
        acc_ref[...] = product()
        for h in range(tn // HEAD_DIM):
            cols = slice(h * HEAD_DIM, (h + 1) * HEAD_DIM)
            z = acc_ref[:, cols]
            ms = jnp.mean(z * z, axis=-1, keepdims=True)
            zn = z * lax.rsqrt(ms + RMS_EPS) * g_ref[...]
            o_ref[:, cols] = zn.astype(o_ref.dtype)
            if write_kv:
                k_out_ref[:, cols] = zn

    def plain(write_kv):
        z = product()
        o_ref[...] = z.astype(o_ref.dtype)
        if write_kv:
            v_out_ref[...] = z

    not_v = jnp.logical_not(is_v)
    no_kv = jnp.logical_not(wants_kv)
    pl.when(jnp.logical_and(not_v, no_kv))(lambda: normed(False))
    pl.when(jnp.logical_and(not_v, wants_kv))(lambda: normed(True))
    pl.when(jnp.logical_and(is_v, no_kv))(lambda: plain(False))
    pl.when(jnp.logical_and(is_v, wants_kv))(lambda: plain(True))


def _mm_vmem(tm, tn, kb, n_lhs_bytes, n_w, out_bytes, extra_tiles):
    x = 2 * tm * n_lhs_bytes
    w = n_w * kb * tn * (2 * 4 + 2)
    o = 2 * tm * tn * out_bytes
    e = extra_tiles * 2 * tm * tn * 4
    return x + w + o + e + 2 * tm * tn * 4


def _mm_plain(x, w, layer, *, tm, tn, kb, kblk, out_dtype):
    m = x.shape[0]
    n = w.shape[2]
    vmem = _mm_vmem(tm, tn, kb, kb * 2, 1, jnp.dtype(out_dtype).itemsize, 0)
    return pl.pallas_call(
        _mm_plain_kernel,
        out_shape=jax.ShapeDtypeStruct((m, n), out_dtype),
        grid=(n // tn, m // tm),
        in_specs=[pl.BlockSpec((tm, kb), lambda j, i: (i, kblk)), _w_spec(kb, tn, layer, kblk)],
        out_specs=pl.BlockSpec((tm, tn), lambda j, i: (i, j)),
        scratch_shapes=[pltpu.VMEM((kb, tn), BF16)],
        compiler_params=_params(vmem, 2),
        name="mm_plain",
    )(x, w)


def _mm_res(xs, ws, layer, xblks, wblks, kb, res, gate, partial=None, *, tm, tn, split_x=False, split_out=False):
    n = ws[0].shape[2]
    n_lhs = len(ws)
    n_p = M_PROMPT // tm
    tile = pl.BlockSpec((tm, tn), lambda j, i: (i, j))
    if split_x:
        kk0 = xblks[0]
        in_specs = [pl.BlockSpec((tm, kb), lambda j, i: (jnp.minimum(i, n_p - 1), kk0)),
                    pl.BlockSpec((tm, kb), lambda j, i: (jnp.maximum(i - n_p, 0), kk0))]
    else:
        in_specs = [pl.BlockSpec((tm, kb), functools.partial(lambda j, i, kk: (i, kk), kk=kk)) for kk in xblks]
    in_specs += [_w_spec(kb, tn, layer, kk) for kk in wblks]
    args = list(xs) + list(ws)
    if partial is not None:
        in_specs.append(tile)
        args.append(partial)
    res_list = list(res) if isinstance(res, tuple) else [res]
    in_specs += list(_split_specs(tm, tn, 2)) if len(res_list) == 2 else [tile]
    in_specs.append(pl.BlockSpec((None, 1, tn), lambda j, i: (_cond_row(i * tm), 0, j)))
    args += res_list + [gate]
    if split_out:
        out_shape = (jax.ShapeDtypeStruct((M_PROMPT, n), F32), jax.ShapeDtypeStruct((M_SAMPLE, n), F32))
        out_specs = _split_specs(tm, tn, 2)
    else:
        out_shape = jax.ShapeDtypeStruct((M_TOTAL, n), F32)
        out_specs = tile
    n_out = 2 if split_out else 1
    vmem = _mm_vmem(tm, tn, kb, len(xs) * kb * 2, n_lhs, 4 * n_out, len(res_list) + (partial is not None))
    return pl.pallas_call(
        functools.partial(_mm_res_kernel, n_lhs=n_lhs, split_x=split_x, has_partial=partial is not None,
                          n_res=len(res_list), n_out=n_out, n_p=n_p),
        out_shape=out_shape,
        grid=(n // tn, M_TOTAL // tm),
        in_specs=in_specs,
        out_specs=out_specs,
        scratch_shapes=[pltpu.VMEM((kb, tn), BF16) for _ in range(n_lhs)],
        compiler_params=_params(vmem, 2),
        name="mm_res",
    )(*args)


def _mm_swiglu(x, w1, w3, layer, *, tm, tn):
    m, k = x.shape
    n = w1.shape[2]
    vmem = _mm_vmem(tm, tn, k, k * 2, 2, 2, 2)
    return pl.pallas_call(
        _mm_swiglu_kernel,
        out_shape=jax.ShapeDtypeStruct((m, n), BF16),
        grid=(n // tn, m // tm),
        in_specs=[pl.BlockSpec((tm, k), lambda j, i: (i, 0)), _w_spec(k, tn, layer, 0), _w_spec(k, tn, layer, 0)],
        out_specs=pl.BlockSpec((tm, tn), lambda j, i: (i, j)),
        scratch_shapes=[pltpu.VMEM((k, tn), BF16), pltpu.VMEM((k, tn), BF16)],
        compiler_params=_params(vmem, 2),
        name="mm_swiglu",
    )(x, w1, w3)


def _mm_qkv(x, w, gains, *, tm, tn):
    m, k = x.shape
    n = w.shape[2]
    tiles_per_section = D_MODEL // tn
    n_p = M_PROMPT // tm

    def k_index(j, i):
        row = jnp.where(j < tiles_per_section, 0,
                        jnp.where(j < 2 * tiles_per_section, jnp.minimum(i, n_p - 1), n_p - 1))
        return row, jnp.clip(j - tiles_per_section, 0, tiles_per_section - 1)

    def v_index(j, i):
        row = jnp.where(j < 2 * tiles_per_section, 0, jnp.minimum(i, n_p - 1))
        return row, jnp.maximum(j - 2 * tiles_per_section, 0)

    vmem = _mm_vmem(tm, tn, k, k * 2, 1, 2, 3)
    prompt_out = jax.ShapeDtypeStruct((M_PROMPT, D_MODEL), F32)
    return pl.pallas_call(
        functools.partial(_mm_qkv_kernel, tn=tn, n_p=n_p),
        out_shape=(jax.ShapeDtypeStruct((m, n), BF16), prompt_out, prompt_out),
        grid=(n // tn, m // tm),
        in_specs=[
            pl.BlockSpec((tm, k), lambda j, i: (i, 0)),
            _w_spec(k, tn, 0, 0),
            pl.BlockSpec((None, 1, HEAD_DIM), lambda j, i: (jnp.minimum(j // tiles_per_section, 1), 0, 0)),
        ],
        out_specs=(pl.BlockSpec((tm, tn), lambda j, i: (i, j)), pl.BlockSpec((tm, tn), k_index),
                   pl.BlockSpec((tm, tn), v_index)),
        scratch_shapes=[pltpu.VMEM((k, tn), BF16), pltpu.VMEM((tm, tn), F32)],
        compiler_params=_params(vmem, 2),
        name="mm_qkv",
    )(x, w, gains)


def _seq_tile_position(i):
    is_prompt = i < M_PROMPT // SEQ_TILE
    tiles_per_seq = DEC_SEQ // SEQ_TILE
    off = jnp.where(is_prompt, 0, (jnp.maximum(i - M_PROMPT // SEQ_TILE, 0) % tiles_per_seq) * SEQ_TILE)
    length = jnp.where(is_prompt, SEQ, DEC_SEQ)
    return off, length


def _halo_specs(width, col_block):
    n_halo_blocks = M_TOTAL // HALO
    per_tile = SEQ_TILE // HALO
    main = pl.BlockSpec((SEQ_TILE, width), lambda i: (i, col_block))
    prev = pl.BlockSpec((HALO, width), lambda i: (jnp.maximum(i * per_tile - 1, 0), col_block))
    nxt = pl.BlockSpec((HALO, width), lambda i: (jnp.minimum((i + 1) * per_tile, n_halo_blocks - 1), col_block))
    return main, prev, nxt


POOL_BUF_ROWS = SEQ_TILE + 3 * HALO


def _pool_kernel(zm_ref, zp_ref, zn_ref, w_ref, s_ref, o_ref, buf_ref, sa_ref, sb_ref, wb_ref):
    i = pl.program_id(0)

    @pl.when(i == 0)
    def _():
        wb_ref[...] = w_ref[...].astype(BF16)

    off, length = _seq_tile_position(i)
    buf_ref[0:HALO, :] = jnp.where(off == 0, 0.0, zp_ref[...])
    buf_ref[HALO:HALO + SEQ_TILE, :] = zm_ref[...]
    buf_ref[HALO + SEQ_TILE:2 * HALO + SEQ_TILE, :] = jnp.where(off + SEQ_TILE == length, 0.0, zn_ref[...])
    buf_ref[2 * HALO + SEQ_TILE:, :] = jnp.zeros((HALO, POOL_WIDTH), F32)

    first = V7X_SUBLANES
    top_end = HALO + SEQ_TILE
    rows = 64
    for g, win in enumerate(POOL_WINDOWS):
        lo, hi = win // 2, win - win // 2 - 1
        cols = slice(g * POOL_GROUP, (g + 1) * POOL_GROUP)
        w_g = wb_ref[g]
        n_levels = min(win.bit_length() - 1, 3)
        ends = [top_end + V7X_SUBLANES * (n_levels - 1 - k) for k in range(n_levels)]
        sa_ref[first:ends[0], :] = buf_ref[first:ends[0], cols] + buf_ref[first + 1:ends[0] + 1, cols]
        top_ref = sa_ref
        if n_levels >= 2:
            sb_ref[first:ends[1], :] = sa_ref[first:ends[1], :] + sa_ref[first + 2:ends[1] + 2, :]
            top_ref = sb_ref
        if n_levels >= 3:
            sa_ref[first:ends[2], :] = sb_ref[first:ends[2], :] + sb_ref[first + 4:ends[2] + 4, :]
            top_ref = sa_ref
        for rc in range(SEQ_TILE // rows):
            r0 = rc * rows
            b0 = HALO + r0 - lo
            acc = top_ref[b0:b0 + rows, :]
            if win > 2 ** n_levels:
                acc = acc + top_ref[b0 + 2 ** n_levels:b0 + 2 ** n_levels + rows, :]
            t = off + r0 + lax.broadcasted_iota(jnp.int32, (rows, 1), 0)
            count = (jnp.minimum(t + hi + 1, length) - jnp.maximum(t - lo, 0)).astype(F32)
            mixed = (acc / count - zm_ref[r0:r0 + rows, cols]).astype(BF16)
            y = jnp.dot(mixed, w_g, preferred_element_type=F32) * s_ref[:, cols]
            o_ref[r0:r0 + rows, cols] = y.astype(o_ref.dtype)


def _pool_mix(z, pool_w, pool_scale):
    main, prev, nxt = _halo_specs(POOL_WIDTH, 0)
    n_g = len(POOL_WINDOWS)
    vmem = (2 * (SEQ_TILE + 2 * HALO) * POOL_WIDTH * 4 + POOL_BUF_ROWS * (POOL_WIDTH + 2 * POOL_GROUP) * 4
            + n_g * POOL_GROUP * POOL_GROUP * (2 * 4 + 2) + 2 * SEQ_TILE * POOL_WIDTH * 2)
    return pl.pallas_call(
        _pool_kernel,
        out_shape=jax.ShapeDtypeStruct((M_TOTAL, POOL_WIDTH), BF16),
        grid=(M_TOTAL // SEQ_TILE,),
        in_specs=[
            main, prev, nxt,
            pl.BlockSpec((n_g, POOL_GROUP, POOL_GROUP), lambda i: (0, 0, 0)),
            pl.BlockSpec((1, POOL_WIDTH), lambda i: (0, 0)),
        ],
        out_specs=pl.BlockSpec((SEQ_TILE, POOL_WIDTH), lambda i: (i, 0)),
        scratch_shapes=[
            pltpu.VMEM((POOL_BUF_ROWS, POOL_WIDTH), F32),
            pltpu.VMEM((POOL_BUF_ROWS, POOL_GROUP), F32),
            pltpu.VMEM((POOL_BUF_ROWS, POOL_GROUP), F32),
            pltpu.VMEM((n_g, POOL_GROUP, POOL_GROUP), BF16),
        ],
        compiler_params=_params(vmem, 1),
        name="pool_mix",
    )(z, z, z, pool_w, pool_scale.reshape(1, POOL_WIDTH))


def _conv_kernel(am_ref, ap_ref, an_ref, gm_ref, gp_ref, gn_ref, w_ref, lg_ref, lb_ref, o_ref, buf_ref, y_ref,
                 win_ref):
    i = pl.program_id(0)
    off, length = _seq_tile_position(i)

    def glu(a_ref, g_ref):
        return a_ref[...] * jax.nn.sigmoid(g_ref[...])

    buf_ref[0:HALO, :] = jnp.where(off == 0, 0.0, glu(ap_ref, gp_ref))
    buf_ref[HALO:HALO + SEQ_TILE, :] = glu(am_ref, gm_ref)
    buf_ref[HALO + SEQ_TILE:, :] = jnp.where(off + SEQ_TILE == length, 0.0, glu(an_ref, gn_ref))

    rows = 128
    first_tap_row = HALO - CONV_K // 2

    def channel_tile(c, carry):
        cols = pl.ds(pl.multiple_of(c * V7X_LANES, V7X_LANES), V7X_LANES)
        taps = [w_ref[k:k + 1, cols] for k in range(CONV_K)]
        for rc in range(SEQ_TILE // rows):
            acc = None
            for phase in range(V7X_SUBLANES):
                ks = [k for k in range(CONV_K) if (first_tap_row + k) % V7X_SUBLANES == phase]
                base = first_tap_row + ks[0] + rc * rows
                n_win = rows + V7X_SUBLANES * (len(ks) - 1)
                win_ref[phase, 0:n_win, :] = buf_ref[base:base + n_win, cols]
                for a, k in enumerate(ks):
                    term = win_ref[phase, V7X_SUBLANES * a:V7X_SUBLANES * a + rows, :] * taps[k]
                    acc = term if acc is None else acc + term
            y_ref[rc * rows:(rc + 1) * rows, cols] = acc
        return carry

    lax.fori_loop(0, CONV_WIDTH // V7X_LANES, channel_tile, 0)

    ln_rows = 64
    for rc in range(SEQ_TILE // ln_rows):
        rs = slice(rc * ln_rows, (rc + 1) * ln_rows)
        y = y_ref[rs, :]
        mu = jnp.mean(y, axis=-1, keepdims=True)
        d = y - mu
        var = jnp.mean(d * d, axis=-1, keepdims=True)
        yn = d * lax.rsqrt(var + LN_EPS) * lg_ref[...] + lb_ref[...]
        o_ref[rs, :] = (yn * jax.nn.sigmoid(yn)).astype(o_ref.dtype)


def _conv_module(z, conv_dw, ln_g, ln_b):
    am, ap, an = _halo_specs(CONV_WIDTH, 1)
    gm, gp, gn = _halo_specs(CONV_WIDTH, 2)
    vec = pl.BlockSpec((1, CONV_WIDTH), lambda i: (0, 0))
    tile_rows = SEQ_TILE + 2 * HALO
    vmem = (2 * 2 * tile_rows * CONV_WIDTH * 4 + tile_rows * CONV_WIDTH * 4 + SEQ_TILE * CONV_WIDTH * 4
            + 2 * SEQ_TILE * CONV_WIDTH * 2 + 4 * SEQ_TILE * CONV_WIDTH * 4)
    return pl.pallas_call(
        _conv_kernel,
        out_shape=jax.ShapeDtypeStruct((M_TOTAL, CONV_WIDTH), BF16),
        grid=(M_TOTAL // SEQ_TILE,),
        in_specs=[am, ap, an, gm, gp, gn, pl.BlockSpec((CONV_K, CONV_WIDTH), lambda i: (0, 0)), vec, vec],
        out_specs=pl.BlockSpec((SEQ_TILE, CONV_WIDTH), lambda i: (i, 0)),
        scratch_shapes=[
            pltpu.VMEM((tile_rows, CONV_WIDTH), F32),
            pltpu.VMEM((SEQ_TILE, CONV_WIDTH), F32),
            pltpu.VMEM((V7X_SUBLANES, 128 + V7X_SUBLANES * (-(-CONV_K // V7X_SUBLANES) - 1), V7X_LANES), F32),
        ],
        compiler_params=_params(vmem, 1),
        name="conv_module",
    )(z, z, z, z, z, z, conv_dw, ln_g.reshape(1, CONV_WIDTH), ln_b.reshape(1, CONV_WIDTH))


_NT_DIMS = (((1,), (1,)), ((), ()))
LOG2_E = float(np.log2(np.e))
Q_SCALE = HEAD_DIM ** -0.5 * LOG2_E
SOFTMAX_ROWS = 16


def _softmax_numerators(s_ref, e_ref, l_ref, n_rows, col0, n_cols, row0=0):
    l_lanes = slice(col0 // n_cols * V7X_LANES, (col0 // n_cols + 1) * V7X_LANES)
    chunks = [slice(row0 + c * SOFTMAX_ROWS, row0 + (c + 1) * SOFTMAX_ROWS) for c in range(n_rows // SOFTMAX_ROWS)]
    for rows in chunks:
        m = jnp.max(s_ref[rows, col0:col0 + n_cols], axis=-1, keepdims=True)
        l_ref[rows, l_lanes] = jnp.broadcast_to(m, (SOFTMAX_ROWS, V7X_LANES))
    for rows in chunks:
        m = l_ref[rows, l_lanes]
        acc = None
        for t in range(n_cols // V7X_LANES):
            lanes = slice(col0 + t * V7X_LANES, col0 + (t + 1) * V7X_LANES)
            e = jnp.exp2(s_ref[rows, lanes] - m)
            e_ref[rows, lanes] = e.astype(BF16)
            acc = e if acc is None else acc + e
        l_ref[rows, l_lanes] = jnp.broadcast_to(jnp.sum(acc, axis=-1, keepdims=True), (SOFTMAX_ROWS, V7X_LANES))


def _ctx_attn_kernel(q_ref, k_ref, v_ref, o_ref, s_ref, e_ref, l_ref, *, heads):
    for h in range(heads):
        cols = slice(h * HEAD_DIM, (h + 1) * HEAD_DIM)
        s_ref[:, h * SEQ:(h + 1) * SEQ] = lax.dot_general(
            q_ref[:, cols], k_ref[:, cols], _NT_DIMS, preferred_element_type=F32)
    for h in range(heads):
        _softmax_numerators(s_ref, e_ref, l_ref, SEQ, h * SEQ, SEQ)
    for h in range(heads):
        cols = slice(h * HEAD_DIM, (h + 1) * HEAD_DIM)
        o = jnp.dot(e_ref[:, h * SEQ:(h + 1) * SEQ], v_ref[:, cols], preferred_element_type=F32) / l_ref[:, cols]
        o_ref[:, cols] = o.astype(o_ref.dtype)


def _ctx_attention(qkv):
    heads = 8
    width = heads * HEAD_DIM
    sections = D_MODEL // width

    def spec(section):
        return pl.BlockSpec((SEQ, width), lambda b, g: (b, section * sections + g))

    vmem = 2 * 4 * SEQ * width * 2 + SEQ * heads * SEQ * 6 + SEQ * width * 4 + 4 * SEQ * heads * SEQ * 4
    return pl.pallas_call(
        functools.partial(_ctx_attn_kernel, heads=heads),
        out_shape=jax.ShapeDtypeStruct((M_PROMPT, D_MODEL), BF16),
        grid=(BATCH, sections),
        in_specs=[spec(0), spec(1), spec(2)],
        out_specs=pl.BlockSpec((SEQ, width), lambda b, g: (b, g)),
        scratch_shapes=[
            pltpu.VMEM((SEQ, heads * SEQ), F32),
            pltpu.VMEM((SEQ, heads * SEQ), BF16),
            pltpu.VMEM((SEQ, width), F32),
        ],
        compiler_params=_params(vmem, 2),
        name="ctx_attention",
    )(qkv, qkv, qkv)


def _window_start(r):
    return min(max(r - WIN_ROWS // 2, 0), GRID_ROWS - WIN_ROWS)


def _na_attn_kernel(q_ref, k_ref, v_ref, kc_ref, vc_ref, t2_ref, o_ref, s_ref, e_ref, l_ref, oc_ref, *, heads):
    n_loc = N_LOCAL_KEYS
    n_keys = n_loc + PAST_LEN

    def head_cols(h):
        return slice(h * HEAD_DIM, (h + 1) * HEAD_DIM)

    def row_slices(r):
        ws = _window_start(r)
        return slice(r * GRID_W, (r + 1) * GRID_W), slice(ws * GRID_W, ws * GRID_W + n_loc), ws

    def ctx_scores(h):
        cols = head_cols(h)
        s_ref[h % 2, :, n_loc:] = lax.dot_general(q_ref[:, cols], kc_ref[h].astype(BF16), _NT_DIMS,
                                                  preferred_element_type=F32)

    def local_scores(h, r):
        cols = head_cols(h)
        rows, keys, ws = row_slices(r)
        s = lax.dot_general(q_ref[rows, cols], k_ref[keys, cols], _NT_DIMS, preferred_element_type=F32)
        d0 = ws - r + WIN_ROWS - 1
        for m in range(n_loc // V7X_LANES):
            lanes = slice(m * V7X_LANES, (m + 1) * V7X_LANES)
            s_ref[h % 2, rows, lanes] = s[:, lanes] + t2_ref[h, d0 + 2 * m]

    def softmax_row(h, r):
        p = h % 2
        _softmax_numerators(s_ref.at[p], e_ref.at[p], l_ref.at[p], GRID_W, 0, n_keys, row0=r * GRID_W)

    def ctx_values(h):
        oc_ref[...] = jnp.dot(e_ref[h % 2, :, n_loc:], vc_ref[h].astype(BF16),
                              preferred_element_type=F32)

    def local_values(h, r):
        cols = head_cols(h)
        rows, keys, _ = row_slices(r)
        o = jnp.dot(e_ref[h % 2, rows, :n_loc], v_ref[keys, cols], preferred_element_type=F32) + oc_ref[rows, :]
        o_ref[rows, cols] = (o / l_ref[h % 2, rows, :]).astype(o_ref.dtype)

    for step in range(heads + 2):
        h_scores, h_softmax, h_values = step, step - 1, step - 2
        if 0 <= h_values < heads:
            ctx_values(h_values)
        if h_scores < heads:
            ctx_scores(h_scores)
        for r in range(GRID_ROWS):
            if 0 <= h_softmax < heads:
                softmax_row(h_softmax, r)
            if h_scores < heads:
                local_scores(h_scores, r)
            if 0 <= h_values < heads:
                local_values(h_values, r)


def _na_attention(qkv, cache_k, cache_v, t2):
    heads = 4
    width = heads * HEAD_DIM
    sections = D_MODEL // width
    row0 = M_PROMPT // DEC_SEQ

    def spec(section):
        return pl.BlockSpec((DEC_SEQ, width), lambda b, g: (row0 + b, section * sections + g))

    cache_spec = pl.BlockSpec((None, heads, PAST_LEN, HEAD_DIM), lambda b, g: (b, g, 0, 0))
    n_keys = N_LOCAL_KEYS + PAST_LEN
    n_t2 = N_REL_ROWS - 1
    vmem = (2 * 4 * DEC_SEQ * width * 2 + 2 * 2 * PAST_LEN * width * 4
            + 2 * heads * n_t2 * GRID_W * V7X_LANES * 4
            + 2 * DEC_SEQ * n_keys * 6 + 3 * DEC_SEQ * V7X_LANES * 4 + 6 * DEC_SEQ * HEAD_DIM * 4)
    return pl.pallas_call(
        functools.partial(_na_attn_kernel, heads=heads),
        out_shape=jax.ShapeDtypeStruct((M_SAMPLE, D_MODEL), BF16),
        grid=(DEC_BATCH, sections),
        in_specs=[
            spec(0), spec(1), spec(2), cache_spec, cache_spec,
            pl.BlockSpec((heads, n_t2, GRID_W, V7X_LANES), lambda b, g: (g, 0, 0, 0)),
        ],
        out_specs=pl.BlockSpec((DEC_SEQ, width), lambda b, g: (b, g)),
        scratch_shapes=[
            pltpu.VMEM((2, DEC_SEQ, n_keys), F32),
            pltpu.VMEM((2, DEC_SEQ, n_keys), BF16),
            pltpu.VMEM((2, DEC_SEQ, V7X_LANES), F32),
            pltpu.VMEM((DEC_SEQ, HEAD_DIM), F32),
        ],
        compiler_params=_params(vmem, 2),
        name="na_attention",
    )(qkv, qkv, qkv, cache_k, cache_v, t2)


def _na_bias_table(rpb):
    n_dc = 2 * WIN_COLS - 1
    qc = np.arange(GRID_W)[:, None]
    kc = np.arange(GRID_W)[None, :]
    cstart = np.clip(qc - WIN_COLS // 2, 0, GRID_W - WIN_COLS)
    col_ok = (kc >= cstart) & (kc < cstart + WIN_COLS)
    dc = np.clip(kc - qc, -(WIN_COLS - 1), WIN_COLS - 1) + WIN_COLS - 1
    one_hot = (dc[None] == np.arange(n_dc)[:, None, None]).astype(np.float32)
    select = np.zeros((2 * n_dc, GRID_W, 2 * GRID_W), np.float32)
    select[:n_dc, :, :GRID_W] = one_hot
    select[n_dc:, :, GRID_W:] = one_hot
    mask = np.where(np.concatenate([col_ok, col_ok], axis=1), 0.0, -np.inf).astype(np.float32)
    pairs = jnp.concatenate([rpb[:, :-1], rpb[:, 1:]], axis=-1) * LOG2_E
    t2 = jnp.einsum("hdk,kqc->hdqc", pairs, jnp.asarray(select), precision=lax.Precision.HIGHEST)
    return t2 + jnp.asarray(mask)


def kernel(x_prompt, x_sample, cache_k_na, cache_v_na, c, c_ctx, ada_w, ada_b, norm1_g, norm2_g, ffn_w1, ffn_w3,
           ffn_w2, pc_w_in, pool_w, pool_scale, conv_dw, conv_ln_g, conv_ln_b, pc_w_out, na_w_qkv, na_q_g, na_k_g,
           na_rpb, na_w_o):
    x_in = (x_prompt.reshape(M_PROMPT, D_MODEL), x_sample.reshape(M_SAMPLE, D_MODEL))
    cond = jnp.concatenate([c_ctx[None], c, jnp.zeros((N_COND - 1 - DEC_BATCH, D_MODEL), F32)], axis=0)
    mod = _ada_mod(cond, ada_w, ada_b)
    mod = mod.reshape(DEPTH, N_COND, 6, D_MODEL).transpose(0, 2, 1, 3).reshape(DEPTH, 6, N_COND, 1, D_MODEL)

    def ffn(x, layer, split_out):
        h = _norm_mod(x, norm2_g[layer], mod[layer, 4], mod[layer, 3])
        hidden = _mm_swiglu(h, ffn_w1, ffn_w3, layer, tm=2048, tn=256)
        half = D_FF // 2
        part = _mm_plain(hidden, ffn_w2, layer, tm=1024, tn=512, kb=half, kblk=0, out_dtype=F32)
        return _mm_res([hidden], [ffn_w2], layer, [1], [1], half, x, mod[layer, 5], part, tm=512, tn=512,
                       split_out=split_out)

    h = _norm_mod(x_in, norm1_g[0], mod[0, 1], mod[0, 0])
    z = _mm_plain(h, pc_w_in, 0, tm=1024, tn=512, kb=D_MODEL, kblk=0, out_dtype=F32)
    y_pool = _pool_mix(z, pool_w[0], pool_scale[0])
    y_conv = _conv_module(z, conv_dw[0], conv_ln_g[0], conv_ln_b[0])
    x = _mm_res([y_pool, y_conv], [pc_w_out, pc_w_out], 0, [0, 0], [0, 1], POOL_WIDTH, x_in, mod[0, 2],
                tm=1024, tn=512)
    x = ffn(x, 0, False)

    h = _norm_mod(x, norm1_g[1], mod[1, 1], mod[1, 0])
    gains = jnp.stack([na_q_g[0] * Q_SCALE, na_k_g[0]]).reshape(2, 1, HEAD_DIM)
    qkv, k_prompt, v_prompt = _mm_qkv(h, na_w_qkv, gains, tm=1024, tn=512)
    new_k = k_prompt.reshape(BATCH, 1, SEQ, N_HEADS, HEAD_DIM)
    new_v = v_prompt.reshape(BATCH, 1, SEQ, N_HEADS, HEAD_DIM)
    cache_k = cache_k_na.reshape(DEC_BATCH, PAST_LEN, N_HEADS, HEAD_DIM).transpose(0, 2, 1, 3)
    cache_v = cache_v_na.reshape(DEC_BATCH, PAST_LEN, N_HEADS, HEAD_DIM).transpose(0, 2, 1, 3)
    attn_sample = _na_attention(qkv, cache_k, cache_v, _na_bias_table(na_rpb[0]))
    attn_prompt = _ctx_attention(qkv)
    x = _mm_res([attn_prompt, attn_sample], [na_w_o], 0, [0], [0], D_MODEL, x, mod[1, 2], tm=1024, tn=512,
                split_x=True)
    y_prompt, y_sample = ffn(x, 1, True)

    return (y_prompt.reshape(BATCH, SEQ, D_MODEL), y_sample.reshape(DEC_BATCH, DEC_SEQ, D_MODEL), new_k, new_v)
```

```python
import functools

import numpy as np
import jax
import jax.numpy as jnp
from jax import lax
from jax.experimental import pallas as pl
from jax.experimental.pallas import tpu as pltpu

F32 = jnp.float32
BF16 = jnp.bfloat16

D_MODEL = 4096
BATCH = 16
SEQ = 256
DEPTH = 2
DEC_BATCH = 8
DEC_SEQ = 1024
PAST_LEN = 512
GRID_W = 64
HEAD_DIM = 128
N_HEADS = D_MODEL // HEAD_DIM
POOL_WIDTH = D_MODEL // 2
CONV_WIDTH = D_MODEL // 2
POOL_WINDOWS = (2, 4, 8, 16)
POOL_GROUP = POOL_WIDTH // len(POOL_WINDOWS)
CONV_K = 31
WIN_ROWS = 8
WIN_COLS = 16
D_FF = 11008
RMS_EPS = 1e-6
LN_EPS = 1e-5

M_PROMPT = BATCH * SEQ
M_SAMPLE = DEC_BATCH * DEC_SEQ
M_TOTAL = M_PROMPT + M_SAMPLE
N_COND = 16
GRID_ROWS = DEC_SEQ // GRID_W
N_LOCAL_KEYS = WIN_ROWS * GRID_W
N_REL_ROWS = 2 * WIN_ROWS - 1

V7X_VMEM_BYTES = 64 * 1024 * 1024
V7X_LANES = 128
V7X_SUBLANES = 8
COMPILER_SCRATCH_BYTES = 6 * 1024 * 1024

HALO = 16
SEQ_TILE = 256


def _params(vmem_bytes, n_axes):
    limit = min(int(vmem_bytes) + COMPILER_SCRATCH_BYTES, V7X_VMEM_BYTES - 2 * 1024 * 1024)
    return pltpu.CompilerParams(dimension_semantics=("arbitrary",) * n_axes, vmem_limit_bytes=limit)


def _cond_row(row_start):
    return jnp.maximum(row_start // DEC_SEQ - (M_PROMPT // DEC_SEQ - 1), 0)


def _split_specs(tm, tn, n_grid_axes):
    n_p = M_PROMPT // tm
    if n_grid_axes == 1:
        return (pl.BlockSpec((tm, tn), lambda i: (jnp.minimum(i, n_p - 1), 0)),
                pl.BlockSpec((tm, tn), lambda i: (jnp.maximum(i - n_p, 0), 0)))
    return (pl.BlockSpec((tm, tn), lambda j, i: (jnp.minimum(i, n_p - 1), j)),
            pl.BlockSpec((tm, tn), lambda j, i: (jnp.maximum(i - n_p, 0), j)))


def _ada_kernel(c_ref, w_ref, b_ref, o_ref):
    c = c_ref[...]
    s = (c * jax.nn.sigmoid(c)).astype(BF16)
    o_ref[...] = jnp.dot(s, w_ref[...].astype(BF16), preferred_element_type=F32) + b_ref[...]


def _ada_mod(cond, ada_w, ada_b):
    tn = 512
    n = 6 * D_MODEL
    vmem = 2 * D_MODEL * tn * 4 + D_MODEL * tn * 2 + 2 * N_COND * D_MODEL * 4
    return pl.pallas_call(
        _ada_kernel,
        out_shape=jax.ShapeDtypeStruct((DEPTH, N_COND, n), F32),
        grid=(DEPTH, n // tn),
        in_specs=[
            pl.BlockSpec((N_COND, D_MODEL), lambda l, j: (0, 0)),
            pl.BlockSpec((None, D_MODEL, tn), lambda l, j: (l, 0, j)),
            pl.BlockSpec((None, 1, tn), lambda l, j: (l, 0, j)),
        ],
        out_specs=pl.BlockSpec((None, N_COND, tn), lambda l, j: (l, 0, j)),
        compiler_params=_params(vmem, 2),
        name="ada_mod",
    )(cond, ada_w, ada_b.reshape(DEPTH, 1, n))


NORM_ROWS = 16


def _norm_mod_rows(x_ref, r_ref, gain_ref, shift_ref, o_ref):
    n_chunks = x_ref.shape[0] // NORM_ROWS
    n_tiles = x_ref.shape[1] // V7X_LANES
    for c in range(n_chunks):
        rows = slice(c * NORM_ROWS, (c + 1) * NORM_ROWS)
        x = x_ref[rows, :]
        ms = jnp.mean(x * x, axis=-1, keepdims=True)
        r_ref[rows, :] = jnp.broadcast_to(lax.rsqrt(ms + RMS_EPS), (NORM_ROWS, V7X_LANES))
    for c in range(n_chunks):
        rows = slice(c * NORM_ROWS, (c + 1) * NORM_ROWS)
        r = r_ref[rows, :]
        for t in range(n_tiles):
            lanes = slice(t * V7X_LANES, (t + 1) * V7X_LANES)
            y = x_ref[rows, lanes] * r * gain_ref[:, lanes] + shift_ref[:, lanes]
            o_ref[rows, lanes] = y.astype(o_ref.dtype)


def _norm_mod_prepare(g_ref, sc_ref, sh_ref, gain_ref, shift_ref):
    gain_ref[...] = jnp.broadcast_to(g_ref[...] * (1.0 + sc_ref[...]), gain_ref.shape)
    shift_ref[...] = jnp.broadcast_to(sh_ref[...], shift_ref.shape)


def _norm_mod_kernel(x_ref, g_ref, sc_ref, sh_ref, o_ref, r_ref, gain_ref, shift_ref):
    _norm_mod_prepare(g_ref, sc_ref, sh_ref, gain_ref, shift_ref)
    _norm_mod_rows(x_ref, r_ref, gain_ref, shift_ref, o_ref)


def _norm_mod_split_kernel(xp_ref, xs_ref, g_ref, sc_ref, sh_ref, o_ref, r_ref, gain_ref, shift_ref, *, n_p):
    _norm_mod_prepare(g_ref, sc_ref, sh_ref, gain_ref, shift_ref)

    @pl.when(pl.program_id(0) < n_p)
    def _():
        _norm_mod_rows(xp_ref, r_ref, gain_ref, shift_ref, o_ref)

    @pl.when(pl.program_id(0) >= n_p)
    def _():
        _norm_mod_rows(xs_ref, r_ref, gain_ref, shift_ref, o_ref)


def _norm_mod(x, g, scale, shift):
    tm = 256
    mod_spec = pl.BlockSpec((None, 1, D_MODEL), lambda i: (_cond_row(i * tm), 0, 0))
    if isinstance(x, tuple):
        body = functools.partial(_norm_mod_split_kernel, n_p=M_PROMPT // tm)
        x_specs = list(_split_specs(tm, D_MODEL, 1))
        xs = list(x)
    else:
        body = _norm_mod_kernel
        x_specs = [pl.BlockSpec((tm, D_MODEL), lambda i: (i, 0))]
        xs = [x]
    vmem = 2 * tm * D_MODEL * (4 * len(xs) + 2) + 4 * tm * D_MODEL * 4
    return pl.pallas_call(
        body,
        out_shape=jax.ShapeDtypeStruct((M_TOTAL, D_MODEL), BF16),
        grid=(M_TOTAL // tm,),
        in_specs=x_specs + [pl.BlockSpec((1, D_MODEL), lambda i: (0, 0)), mod_spec, mod_spec],
        out_specs=pl.BlockSpec((tm, D_MODEL), lambda i: (i, 0)),
        scratch_shapes=[
            pltpu.VMEM((tm, V7X_LANES), F32),
            pltpu.VMEM((NORM_ROWS, D_MODEL), F32),
            pltpu.VMEM((NORM_ROWS, D_MODEL), F32),
        ],
        compiler_params=_params(vmem, 1),
        name="norm_mod",
    )(*xs, g.reshape(1, D_MODEL), scale, shift)


def _cast_weights_once(w_refs, wb_refs):
    @pl.when(pl.program_id(1) == 0)
    def _():
        for w_ref, wb_ref in zip(w_refs, wb_refs):
            wb_ref[...] = w_ref[...].astype(BF16)


def _w_spec(kb, tn, layer, kblk, single_buffer=False):
    mode = pl.Buffered(1) if single_buffer else None
    return pl.BlockSpec((None, kb, tn), lambda j, i: (layer, kblk, j), pipeline_mode=mode)


def _mm_plain_kernel(x_ref, w_ref, o_ref, wb_ref):
    _cast_weights_once([w_ref], [wb_ref])
    o_ref[...] = jnp.dot(x_ref[...], wb_ref[...], preferred_element_type=F32).astype(o_ref.dtype)


def _mm_res_kernel(*refs, n_lhs, split_x, has_partial, n_res, n_out, n_p):
    refs = list(refs)
    x_refs = [refs.pop(0) for _ in range(2 if split_x else n_lhs)]
    w_refs = [refs.pop(0) for _ in range(n_lhs)]
    p_ref = refs.pop(0) if has_partial else None
    res_refs = [refs.pop(0) for _ in range(n_res)]
    gate_ref = refs.pop(0)
    out_refs = [refs.pop(0) for _ in range(n_out)]
    wb_refs = refs
    _cast_weights_once(w_refs, wb_refs)
    is_prompt = pl.program_id(1) < n_p

    def body(lhs_refs, side):
        acc = jnp.dot(lhs_refs[0][...], wb_refs[0][...], preferred_element_type=F32)
        for x_ref, wb_ref in zip(lhs_refs[1:], wb_refs[1:]):
            acc = acc + jnp.dot(x_ref[...], wb_ref[...], preferred_element_type=F32)
        if has_partial:
            acc = acc + p_ref[...]
        if n_res == 1:
            res = res_refs[0][...]
        elif side is None:
            res = jnp.where(is_prompt, res_refs[0][...], res_refs[1][...])
        else:
            res = res_refs[side][...]
        y = res + gate_ref[...] * acc
        if n_out == 1:
            out_refs[0][...] = y
        elif side is None:
            @pl.when(is_prompt)
            def _():
                out_refs[0][...] = y

            @pl.when(jnp.logical_not(is_prompt))
            def _():
                out_refs[1][...] = y
        else:
            out_refs[side][...] = y

    if split_x:
        @pl.when(is_prompt)
        def _():
            body([x_refs[0]], 0)

        @pl.when(jnp.logical_not(is_prompt))
        def _():
            body([x_refs[1]], 1)
    else:
        body(x_refs, None)


def _mm_swiglu_kernel(x_ref, w1_ref, w3_ref, o_ref, wb1_ref, wb3_ref):
    _cast_weights_once([w1_ref, w3_ref], [wb1_ref, wb3_ref])
    x = x_ref[...]
    a = jnp.dot(x, wb1_ref[...], preferred_element_type=F32)
    b = jnp.dot(x, wb3_ref[...], preferred_element_type=F32)
    o_ref[...] = (a * jax.nn.sigmoid(a) * b).astype(o_ref.dtype)


def _mm_qkv_kernel(x_ref, w_ref, g_ref, o_ref, k_out_ref, v_out_ref, wb_ref, acc_ref, *, tn, n_p):
    _cast_weights_once([w_ref], [wb_ref])
    j, i = pl.program_id(0), pl.program_id(1)
    tiles_per_section = D_MODEL // tn
    is_v = j >= 2 * tiles_per_section
    wants_kv = jnp.logical_and(j >= tiles_per_section, i < n_p)

    def product():
        return jnp.dot(x_ref[...], wb_ref[...], preferred_element_type=F32)

    def normed(write_kv):
        acc_ref[...] = product()
        for h in range(tn // HEAD_DIM):
            cols = slice(h * HEAD_DIM, (h + 1) * HEAD_DIM)
            z = acc_ref[:, cols]
            ms = jnp.mean(z * z, axis=-1, keepdims=True)
            zn = z * lax.rsqrt(ms + RMS_EPS) * g_ref[...]
            o_ref[:, cols] = zn.astype(o_ref.dtype)
            if write_kv:
                k_out_ref[:, cols] = zn

    def plain(write_kv):
        z = product()
        o_ref[...] = z.astype(o_ref.dtype)
        if write_kv:
            v_out_ref[...] = z

    not_v = jnp.logical_not(is_v)
    no_kv = jnp.logical_not(wants_kv)
    pl.when(jnp.logical_and(not_v, no_kv))(lambda: normed(False))
    pl.when(jnp.logical_and(not_v, wants_kv))(lambda: normed(True))
    pl.when(jnp.logical_and(is_v, no_kv))(lambda: plain(False))
    pl.when(jnp.logical_and(is_v, wants_kv))(lambda: plain(True))


def _mm_vmem(tm, tn, kb, n_lhs_bytes, n_w, out_bytes, extra_tiles):
    x = 2 * tm * n_lhs_bytes
    w = n_w * kb * tn * (2 * 4 + 2)
    o = 2 * tm * tn * out_bytes
    e = extra_tiles * 2 * tm * tn * 4
    return x + w + o + e + 2 * tm * tn * 4


def _mm_plain(x, w, layer, *, tm, tn, kb, kblk, out_dtype):
    m = x.shape[0]
    n = w.shape[2]
    vmem = _mm_vmem(tm, tn, kb, kb * 2, 1, jnp.dtype(out_dtype).itemsize, 0)
    return pl.pallas_call(
        _mm_plain_kernel,
        out_shape=jax.ShapeDtypeStruct((m, n), out_dtype),
        grid=(n // tn, m // tm),
        in_specs=[pl.BlockSpec((tm, kb), lambda j, i: (i, kblk)), _w_spec(kb, tn, layer, kblk)],
        out_specs=pl.BlockSpec((tm, tn), lambda j, i: (i, j)),
        scratch_shapes=[pltpu.VMEM((kb, tn), BF16)],
        compiler_params=_params(vmem, 2),
        name="mm_plain",
    )(x, w)


def _mm_res(xs, ws, layer, xblks, wblks, kb, res, gate, partial=None, *, tm, tn, split_x=False, split_out=False,
            single_buffer_w=False):
    n = ws[0].shape[2]
    n_lhs = len(ws)
    n_p = M_PROMPT // tm
    tile = pl.BlockSpec((tm, tn), lambda j, i: (i, j))
    if split_x:
        kk0 = xblks[0]
        in_specs = [pl.BlockSpec((tm, kb), lambda j, i: (jnp.minimum(i, n_p - 1), kk0)),
                    pl.BlockSpec((tm, kb), lambda j, i: (jnp.maximum(i - n_p, 0), kk0))]
    else:
        in_specs = [pl.BlockSpec((tm, kb), functools.partial(lambda j, i, kk: (i, kk), kk=kk)) for kk in xblks]
    in_specs += [_w_spec(kb, tn, layer, kk, single_buffer_w) for kk in wblks]
    args = list(xs) + list(ws)
    if partial is not None:
        in_specs.append(tile)
        args.append(partial)
    res_list = list(res) if isinstance(res, tuple) else [res]
    in_specs += list(_split_specs(tm, tn, 2)) if len(res_list) == 2 else [tile]
    in_specs.append(pl.BlockSpec((None, 1, tn), lambda j, i: (_cond_row(i * tm), 0, j)))
    args += res_list + [gate]
    if split_out:
        out_shape = (jax.ShapeDtypeStruct((M_PROMPT, n), F32), jax.ShapeDtypeStruct((M_SAMPLE, n), F32))
        out_specs = _split_specs(tm, tn, 2)
    else:
        out_shape = jax.ShapeDtypeStruct((M_TOTAL, n), F32)
        out_specs = tile
    n_out = 2 if split_out else 1
    vmem = _mm_vmem(tm, tn, kb, len(xs) * kb * 2, n_lhs, 4 * n_out, len(res_list) + (partial is not None))
    return pl.pallas_call(
        functools.partial(_mm_res_kernel, n_lhs=n_lhs, split_x=split_x, has_partial=partial is not None,
                          n_res=len(res_list), n_out=n_out, n_p=n_p),
        out_shape=out_shape,
        grid=(n // tn, M_TOTAL // tm),
        in_specs=in_specs,
        out_specs=out_specs,
        scratch_shapes=[pltpu.VMEM((kb, tn), BF16) for _ in range(n_lhs)],
        compiler_params=_params(vmem, 2),
        name="mm_res",
    )(*args)


def _mm_swiglu(x, w1, w3, layer, *, tm, tn):
    m, k = x.shape
    n = w1.shape[2]
    vmem = _mm_vmem(tm, tn, k, k * 2, 2, 2, 2)
    return pl.pallas_call(
        _mm_swiglu_kernel,
        out_shape=jax.ShapeDtypeStruct((m, n), BF16),
        grid=(n // tn, m // tm),
        in_specs=[pl.BlockSpec((tm, k), lambda j, i: (i, 0)), _w_spec(k, tn, layer, 0), _w_spec(k, tn, layer, 0)],
        out_specs=pl.BlockSpec((tm, tn), lambda j, i: (i, j)),
        scratch_shapes=[pltpu.VMEM((k, tn), BF16), pltpu.VMEM((k, tn), BF16)],
        compiler_params=_params(vmem, 2),
        name="mm_swiglu",
    )(x, w1, w3)


def _mm_qkv(x, w, gains, *, tm, tn):
    m, k = x.shape
    n = w.shape[2]
    tiles_per_section = D_MODEL // tn
    n_p = M_PROMPT // tm

    def k_index(j, i):
        row = jnp.where(j < tiles_per_section, 0,
                        jnp.where(j < 2 * tiles_per_section, jnp.minimum(i, n_p - 1), n_p - 1))
        return row, jnp.clip(j - tiles_per_section, 0, tiles_per_section - 1)

    def v_index(j, i):
        row = jnp.where(j < 2 * tiles_per_section, 0, jnp.minimum(i, n_p - 1))
        return row, jnp.maximum(j - 2 * tiles_per_section, 0)

    vmem = _mm_vmem(tm, tn, k, k * 2, 1, 2, 3)
    prompt_out = jax.ShapeDtypeStruct((M_PROMPT, D_MODEL), F32)
    return pl.pallas_call(
        functools.partial(_mm_qkv_kernel, tn=tn, n_p=n_p),
        out_shape=(jax.ShapeDtypeStruct((m, n), BF16), prompt_out, prompt_out),
        grid=(n // tn, m // tm),
        in_specs=[
            pl.BlockSpec((tm, k), lambda j, i: (i, 0)),
            _w_spec(k, tn, 0, 0),
            pl.BlockSpec((None, 1, HEAD_DIM), lambda j, i: (jnp.minimum(j // tiles_per_section, 1), 0, 0)),
        ],
        out_specs=(pl.BlockSpec((tm, tn), lambda j, i: (i, j)), pl.BlockSpec((tm, tn), k_index),
                   pl.BlockSpec((tm, tn), v_index)),
        scratch_shapes=[pltpu.VMEM((k, tn), BF16), pltpu.VMEM((tm, tn), F32)],
        compiler_params=_params(vmem, 2),
        name="mm_qkv",
    )(x, w, gains)


def _seq_tile_position(i):
    is_prompt = i < M_PROMPT // SEQ_TILE
    tiles_per_seq = DEC_SEQ // SEQ_TILE
    off = jnp.where(is_prompt, 0, (jnp.maximum(i - M_PROMPT // SEQ_TILE, 0) % tiles_per_seq) * SEQ_TILE)
    length = jnp.where(is_prompt, SEQ, DEC_SEQ)
    return off, length


def _halo_specs(width, col_block):
    n_halo_blocks = M_TOTAL // HALO
    per_tile = SEQ_TILE // HALO
    main = pl.BlockSpec((SEQ_TILE, width), lambda i: (i, col_block))
    prev = pl.BlockSpec((HALO, width), lambda i: (jnp.maximum(i * per_tile - 1, 0), col_block))
    nxt = pl.BlockSpec((HALO, width), lambda i: (jnp.minimum((i + 1) * per_tile, n_halo_blocks - 1), col_block))
    return main, prev, nxt


POOL_BUF_ROWS = SEQ_TILE + 3 * HALO


def _pool_kernel(zm_ref, zp_ref, zn_ref, w_ref, s_ref, o_ref, buf_ref, sa_ref, sb_ref, wb_ref):
    i = pl.program_id(0)

    @pl.when(i == 0)
    def _():
        wb_ref[...] = w_ref[...].astype(BF16)

    off, length = _seq_tile_position(i)
    buf_ref[0:HALO, :] = jnp.where(off == 0, 0.0, zp_ref[...])
    buf_ref[HALO:HALO + SEQ_TILE, :] = zm_ref[...]
    buf_ref[HALO + SEQ_TILE:2 * HALO + SEQ_TILE, :] = jnp.where(off + SEQ_TILE == length, 0.0, zn_ref[...])
    buf_ref[2 * HALO + SEQ_TILE:, :] = jnp.zeros((HALO, POOL_WIDTH), F32)

    first = V7X_SUBLANES
    top_end = HALO + SEQ_TILE
    rows = 64
    for g, win in enumerate(POOL_WINDOWS):
        lo, hi = win // 2, win - win // 2 - 1
        cols = slice(g * POOL_GROUP, (g + 1) * POOL_GROUP)
        w_g = wb_ref[g]
        n_levels = min(win.bit_length() - 1, 3)
        ends = [top_end + V7X_SUBLANES * (n_levels - 1 - k) for k in range(n_levels)]
        sa_ref[first:ends[0], :] = buf_ref[first:ends[0], cols] + buf_ref[first + 1:ends[0] + 1, cols]
        top_ref = sa_ref
        if n_levels >= 2:
            sb_ref[first:ends[1], :] = sa_ref[first:ends[1], :] + sa_ref[first + 2:ends[1] + 2, :]
            top_ref = sb_ref
        if n_levels >= 3:
            sa_ref[first:ends[2], :] = sb_ref[first:ends[2], :] + sb_ref[first + 4:ends[2] + 4, :]
            top_ref = sa_ref
        for rc in range(SEQ_TILE // rows):
            r0 = rc * rows
            b0 = HALO + r0 - lo
            acc = top_ref[b0:b0 + rows, :]
            if win > 2 ** n_levels:
                acc = acc + top_ref[b0 + 2 ** n_levels:b0 + 2 ** n_levels + rows, :]
            t = off + r0 + lax.broadcasted_iota(jnp.int32, (rows, 1), 0)
            count = (jnp.minimum(t + hi + 1, length) - jnp.maximum(t - lo, 0)).astype(F32)
            mixed = (acc / count - zm_ref[r0:r0 + rows, cols]).astype(BF16)
            y = jnp.dot(mixed, w_g, preferred_element_type=F32) * s_ref[:, cols]
            o_ref[r0:r0 + rows, cols] = y.astype(o_ref.dtype)


def _pool_mix(z, pool_w, pool_scale):
    main, prev, nxt = _halo_specs(POOL_WIDTH, 0)
    n_g = len(POOL_WINDOWS)
    vmem = (2 * (SEQ_TILE + 2 * HALO) * POOL_WIDTH * 4 + POOL_BUF_ROWS * (POOL_WIDTH + 2 * POOL_GROUP) * 4
            + n_g * POOL_GROUP * POOL_GROUP * (2 * 4 + 2) + 2 * SEQ_TILE * POOL_WIDTH * 2)
    return pl.pallas_call(
        _pool_kernel,
        out_shape=jax.ShapeDtypeStruct((M_TOTAL, POOL_WIDTH), BF16),
        grid=(M_TOTAL // SEQ_TILE,),
        in_specs=[
            main, prev, nxt,
            pl.BlockSpec((n_g, POOL_GROUP, POOL_GROUP), lambda i: (0, 0, 0)),
            pl.BlockSpec((1, POOL_WIDTH), lambda i: (0, 0)),
        ],
        out_specs=pl.BlockSpec((SEQ_TILE, POOL_WIDTH), lambda i: (i, 0)),
        scratch_shapes=[
            pltpu.VMEM((POOL_BUF_ROWS, POOL_WIDTH), F32),
            pltpu.VMEM((POOL_BUF_ROWS, POOL_GROUP), F32),
            pltpu.VMEM((POOL_BUF_ROWS, POOL_GROUP), F32),
            pltpu.VMEM((n_g, POOL_GROUP, POOL_GROUP), BF16),
        ],
        compiler_params=_params(vmem, 1),
        name="pool_mix",
    )(z, z, z, pool_w, pool_scale.reshape(1, POOL_WIDTH))


def _conv_kernel(am_ref, ap_ref, an_ref, gm_ref, gp_ref, gn_ref, w_ref, lg_ref, lb_ref, o_ref, buf_ref, y_ref,
                 win_ref):
    i = pl.program_id(0)
    off, length = _seq_tile_position(i)

    def glu(a_ref, g_ref):
        return a_ref[...] * jax.nn.sigmoid(g_ref[...])

    buf_ref[0:HALO, :] = jnp.where(off == 0, 0.0, glu(ap_ref, gp_ref))
    buf_ref[HALO:HALO + SEQ_TILE, :] = glu(am_ref, gm_ref)
    buf_ref[HALO + SEQ_TILE:, :] = jnp.where(off + SEQ_TILE == length, 0.0, glu(an_ref, gn_ref))

    rows = 128
    first_tap_row = HALO - CONV_K // 2

    def channel_tile(c, carry):
        cols = pl.ds(pl.multiple_of(c * V7X_LANES, V7X_LANES), V7X_LANES)
        taps = [w_ref[k:k + 1, cols] for k in range(CONV_K)]
        for rc in range(SEQ_TILE // rows):
            acc = None
            for phase in range(V7X_SUBLANES):
                ks = [k for k in range(CONV_K) if (first_tap_row + k) % V7X_SUBLANES == phase]
                base = first_tap_row + ks[0] + rc * rows
                n_win = rows + V7X_SUBLANES * (len(ks) - 1)
                win_ref[phase, 0:n_win, :] = buf_ref[base:base + n_win, cols]
                for a, k in enumerate(ks):
                    term = win_ref[phase, V7X_SUBLANES * a:V7X_SUBLANES * a + rows, :] * taps[k]
                    acc = term if acc is None else acc + term
            y_ref[rc * rows:(rc + 1) * rows, cols] = acc
        return carry

    lax.fori_loop(0, CONV_WIDTH // V7X_LANES, channel_tile, 0)

    ln_rows = 64
    for rc in range(SEQ_TILE // ln_rows):
        rs = slice(rc * ln_rows, (rc + 1) * ln_rows)
        y = y_ref[rs, :]
        mu = jnp.mean(y, axis=-1, keepdims=True)
        d = y - mu
        var = jnp.mean(d * d, axis=-1, keepdims=True)
        yn = d * lax.rsqrt(var + LN_EPS) * lg_ref[...] + lb_ref[...]
        o_ref[rs, :] = (yn * jax.nn.sigmoid(yn)).astype(o_ref.dtype)


def _conv_module(z, conv_dw, ln_g, ln_b):
    am, ap, an = _halo_specs(CONV_WIDTH, 1)
    gm, gp, gn = _halo_specs(CONV_WIDTH, 2)
    vec = pl.BlockSpec((1, CONV_WIDTH), lambda i: (0, 0))
    tile_rows = SEQ_TILE + 2 * HALO
    vmem = (2 * 2 * tile_rows * CONV_WIDTH * 4 + tile_rows * CONV_WIDTH * 4 + SEQ_TILE * CONV_WIDTH * 4
            + 2 * SEQ_TILE * CONV_WIDTH * 2 + 4 * SEQ_TILE * CONV_WIDTH * 4)
    return pl.pallas_call(
        _conv_kernel,
        out_shape=jax.ShapeDtypeStruct((M_TOTAL, CONV_WIDTH), BF16),
        grid=(M_TOTAL // SEQ_TILE,),
        in_specs=[am, ap, an, gm, gp, gn, pl.BlockSpec((CONV_K, CONV_WIDTH), lambda i: (0, 0)), vec, vec],
        out_specs=pl.BlockSpec((SEQ_TILE, CONV_WIDTH), lambda i: (i, 0)),
        scratch_shapes=[
            pltpu.VMEM((tile_rows, CONV_WIDTH), F32),
            pltpu.VMEM((SEQ_TILE, CONV_WIDTH), F32),
            pltpu.VMEM((V7X_SUBLANES, 128 + V7X_SUBLANES * (-(-CONV_K // V7X_SUBLANES) - 1), V7X_LANES), F32),
        ],
        compiler_params=_params(vmem, 1),
        name="conv_module",
    )(z, z, z, z, z, z, conv_dw, ln_g.reshape(1, CONV_WIDTH), ln_b.reshape(1, CONV_WIDTH))


_NT_DIMS = (((1,), (1,)), ((), ()))
LOG2_E = float(np.log2(np.e))
Q_SCALE = HEAD_DIM ** -0.5 * LOG2_E
SOFTMAX_ROWS = 16


def _softmax_numerators(s_ref, e_ref, l_ref, n_rows, col0, n_cols, row0=0):
    l_lanes = slice(col0 // n_cols * V7X_LANES, (col0 // n_cols + 1) * V7X_LANES)
    chunks = [slice(row0 + c * SOFTMAX_ROWS, row0 + (c + 1) * SOFTMAX_ROWS) for c in range(n_rows // SOFTMAX_ROWS)]
    for rows in chunks:
        m = jnp.max(s_ref[rows, col0:col0 + n_cols], axis=-1, keepdims=True)
        l_ref[rows, l_lanes] = jnp.broadcast_to(m, (SOFTMAX_ROWS, V7X_LANES))
    for rows in chunks:
        m = l_ref[rows, l_lanes]
        acc = None
        for t in range(n_cols // V7X_LANES):
            lanes = slice(col0 + t * V7X_LANES, col0 + (t + 1) * V7X_LANES)
            e = jnp.exp2(s_ref[rows, lanes] - m)
            e_ref[rows, lanes] = e.astype(BF16)
            acc = e if acc is None else acc + e
        l_ref[rows, l_lanes] = jnp.broadcast_to(jnp.sum(acc, axis=-1, keepdims=True), (SOFTMAX_ROWS, V7X_LANES))


def _ctx_attn_kernel(q_ref, k_ref, v_ref, o_ref, s_ref, e_ref, l_ref, *, heads):
    for h in range(heads):
        cols = slice(h * HEAD_DIM, (h + 1) * HEAD_DIM)
        s_ref[:, h * SEQ:(h + 1) * SEQ] = lax.dot_general(
            q_ref[:, cols], k_ref[:, cols], _NT_DIMS, preferred_element_type=F32)
    for h in range(heads):
        _softmax_numerators(s_ref, e_ref, l_ref, SEQ, h * SEQ, SEQ)
    for h in range(heads):
        cols = slice(h * HEAD_DIM, (h + 1) * HEAD_DIM)
        o = jnp.dot(e_ref[:, h * SEQ:(h + 1) * SEQ], v_ref[:, cols], preferred_element_type=F32) / l_ref[:, cols]
        o_ref[:, cols] = o.astype(o_ref.dtype)


def _ctx_attention(qkv):
    heads = 8
    width = heads * HEAD_DIM
    sections = D_MODEL // width

    def spec(section):
        return pl.BlockSpec((SEQ, width), lambda b, g: (b, section * sections + g))

    vmem = 2 * 4 * SEQ * width * 2 + SEQ * heads * SEQ * 6 + SEQ * width * 4 + 4 * SEQ * heads * SEQ * 4
    return pl.pallas_call(
        functools.partial(_ctx_attn_kernel, heads=heads),
        out_shape=jax.ShapeDtypeStruct((M_PROMPT, D_MODEL), BF16),
        grid=(BATCH, sections),
        in_specs=[spec(0), spec(1), spec(2)],
        out_specs=pl.BlockSpec((SEQ, width), lambda b, g: (b, g)),
        scratch_shapes=[
            pltpu.VMEM((SEQ, heads * SEQ), F32),
            pltpu.VMEM((SEQ, heads * SEQ), BF16),
            pltpu.VMEM((SEQ, width), F32),
        ],
        compiler_params=_params(vmem, 2),
        name="ctx_attention",
    )(qkv, qkv, qkv)


def _window_start(r):
    return min(max(r - WIN_ROWS // 2, 0), GRID_ROWS - WIN_ROWS)


def _na_attn_kernel(q_ref, k_ref, v_ref, kc_ref, vc_ref, t2_ref, o_ref, s_ref, e_ref, l_ref, oc_ref, *, heads):
    n_loc = N_LOCAL_KEYS
    n_keys = n_loc + PAST_LEN

    def head_cols(h):
        return slice(h * HEAD_DIM, (h + 1) * HEAD_DIM)

    def row_slices(r):
        ws = _window_start(r)
        return slice(r * GRID_W, (r + 1) * GRID_W), slice(ws * GRID_W, ws * GRID_W + n_loc), ws

    def ctx_scores(h):
        cols = head_cols(h)
        s_ref[h % 2, :, n_loc:] = lax.dot_general(q_ref[:, cols], kc_ref[h].astype(BF16), _NT_DIMS,
                                                  preferred_element_type=F32)

    def local_scores(h, r):
        cols = head_cols(h)
        rows, keys, ws = row_slices(r)
        s = lax.dot_general(q_ref[rows, cols], k_ref[keys, cols], _NT_DIMS, preferred_element_type=F32)
        d0 = ws - r + WIN_ROWS - 1
        for m in range(n_loc // V7X_LANES):
            lanes = slice(m * V7X_LANES, (m + 1) * V7X_LANES)
            s_ref[h % 2, rows, lanes] = s[:, lanes] + t2_ref[h, d0 + 2 * m]

    def softmax_row(h, r):
        p = h % 2
        _softmax_numerators(s_ref.at[p], e_ref.at[p], l_ref.at[p], GRID_W, 0, n_keys, row0=r * GRID_W)

    def ctx_values(h):
        oc_ref[...] = jnp.dot(e_ref[h % 2, :, n_loc:], vc_ref[h].astype(BF16),
                              preferred_element_type=F32)

    def local_values(h, r):
        cols = head_cols(h)
        rows, keys, _ = row_slices(r)
        o = jnp.dot(e_ref[h % 2, rows, :n_loc], v_ref[keys, cols], preferred_element_type=F32) + oc_ref[rows, :]
        o_ref[rows, cols] = (o / l_ref[h % 2, rows, :]).astype(o_ref.dtype)

    for step in range(heads + 2):
        h_scores, h_softmax, h_values = step, step - 1, step - 2
        if 0 <= h_values < heads:
            ctx_values(h_values)
        if h_scores < heads:
            ctx_scores(h_scores)
        for r in range(GRID_ROWS):
            if 0 <= h_softmax < heads:
                softmax_row(h_softmax, r)
            if h_scores < heads:
                local_scores(h_scores, r)
            if 0 <= h_values < heads:
                local_values(h_values, r)


def _na_attention(qkv, cache_k, cache_v, t2):
    heads = 4
    width = heads * HEAD_DIM
    sections = D_MODEL // width
    row0 = M_PROMPT // DEC_SEQ

    def spec(section):
        return pl.BlockSpec((DEC_SEQ, width), lambda b, g: (row0 + b, section * sections + g))

    cache_spec = pl.BlockSpec((None, heads, PAST_LEN, HEAD_DIM), lambda b, g: (b, g, 0, 0))
    n_keys = N_LOCAL_KEYS + PAST_LEN
    n_t2 = N_REL_ROWS - 1
    vmem = (2 * 4 * DEC_SEQ * width * 2 + 2 * 2 * PAST_LEN * width * 4
            + 2 * heads * n_t2 * GRID_W * V7X_LANES * 4
            + 2 * DEC_SEQ * n_keys * 6 + 3 * DEC_SEQ * V7X_LANES * 4 + 6 * DEC_SEQ * HEAD_DIM * 4)
    return pl.pallas_call(
        functools.partial(_na_attn_kernel, heads=heads),
        out_shape=jax.ShapeDtypeStruct((M_SAMPLE, D_MODEL), BF16),
        grid=(DEC_BATCH, sections),
        in_specs=[
            spec(0), spec(1), spec(2), cache_spec, cache_spec,
            pl.BlockSpec((heads, n_t2, GRID_W, V7X_LANES), lambda b, g: (g, 0, 0, 0)),
        ],
        out_specs=pl.BlockSpec((DEC_SEQ, width), lambda b, g: (b, g)),
        scratch_shapes=[
            pltpu.VMEM((2, DEC_SEQ, n_keys), F32),
            pltpu.VMEM((2, DEC_SEQ, n_keys), BF16),
            pltpu.VMEM((2, DEC_SEQ, V7X_LANES), F32),
            pltpu.VMEM((DEC_SEQ, HEAD_DIM), F32),
        ],
        compiler_params=_params(vmem, 2),
        name="na_attention",
    )(qkv, qkv, qkv, cache_k, cache_v, t2)


def _na_bias_table(rpb):
    n_dc = 2 * WIN_COLS - 1
    qc = np.arange(GRID_W)[:, None]
    kc = np.arange(GRID_W)[None, :]
    cstart = np.clip(qc - WIN_COLS // 2, 0, GRID_W - WIN_COLS)
    col_ok = (kc >= cstart) & (kc < cstart + WIN_COLS)
    dc = np.clip(kc - qc, -(WIN_COLS - 1), WIN_COLS - 1) + WIN_COLS - 1
    one_hot = (dc[None] == np.arange(n_dc)[:, None, None]).astype(np.float32)
    select = np.zeros((2 * n_dc, GRID_W, 2 * GRID_W), np.float32)
    select[:n_dc, :, :GRID_W] = one_hot
    select[n_dc:, :, GRID_W:] = one_hot
    mask = np.where(np.concatenate([col_ok, col_ok], axis=1), 0.0, -np.inf).astype(np.float32)
    pairs = jnp.concatenate([rpb[:, :-1], rpb[:, 1:]], axis=-1) * LOG2_E
    t2 = jnp.einsum("hdk,kqc->hdqc", pairs, jnp.asarray(select), precision=lax.Precision.HIGHEST)
    return t2 + jnp.asarray(mask)


def kernel(x_prompt, x_sample, cache_k_na, cache_v_na, c, c_ctx, ada_w, ada_b, norm1_g, norm2_g, ffn_w1, ffn_w3,
           ffn_w2, pc_w_in, pool_w, pool_scale, conv_dw, conv_ln_g, conv_ln_b, pc_w_out, na_w_qkv, na_q_g, na_k_g,
           na_rpb, na_w_o):
    x_in = (x_prompt.reshape(M_PROMPT, D_MODEL), x_sample.reshape(M_SAMPLE, D_MODEL))
    cond = jnp.concatenate([c_ctx[None], c, jnp.zeros((N_COND - 1 - DEC_BATCH, D_MODEL), F32)], axis=0)
    mod = _ada_mod(cond, ada_w, ada_b)
    mod = mod.reshape(DEPTH, N_COND, 6, D_MODEL).transpose(0, 2, 1, 3).reshape(DEPTH, 6, N_COND, 1, D_MODEL)

    def ffn(x, layer, split_out):
        h = _norm_mod(x, norm2_g[layer], mod[layer, 4], mod[layer, 3])
        hidden = _mm_swiglu(h, ffn_w1, ffn_w3, layer, tm=2048, tn=256)
        half = D_FF // 2
        part = _mm_plain(hidden, ffn_w2, layer, tm=1024, tn=512, kb=half, kblk=0, out_dtype=F32)
        return _mm_res([hidden], [ffn_w2], layer, [1], [1], half, x, mod[layer, 5], part, tm=1024, tn=512,
                       split_out=split_out, single_buffer_w=True)

    h = _norm_mod(x_in, norm1_g[0], mod[0, 1], mod[0, 0])
    z = _mm_plain(h, pc_w_in, 0, tm=1024, tn=512, kb=D_MODEL, kblk=0, out_dtype=F32)
    y_pool = _pool_mix(z, pool_w[0], pool_scale[0])
    y_conv = _conv_module(z, conv_dw[0], conv_ln_g[0], conv_ln_b[0])
    x = _mm_res([y_pool, y_conv], [pc_w_out, pc_w_out], 0, [0, 0], [0, 1], POOL_WIDTH, x_in, mod[0, 2],
                tm=1024, tn=512)
    x = ffn(x, 0, False)

    h = _norm_mod(x, norm1_g[1], mod[1, 1], mod[1, 0])
    gains = jnp.stack([na_q_g[0] * Q_SCALE, na_k_g[0]]).reshape(2, 1, HEAD_DIM)
    qkv, k_prompt, v_prompt = _mm_qkv(h, na_w_qkv, gains, tm=1024, tn=512)
    new_k = k_prompt.reshape(BATCH, 1, SEQ, N_HEADS, HEAD_DIM)
    new_v = v_prompt.reshape(BATCH, 1, SEQ, N_HEADS, HEAD_DIM)
    cache_k = cache_k_na.reshape(DEC_BATCH, PAST_LEN, N_HEADS, HEAD_DIM).transpose(0, 2, 1, 3)
    cache_v = cache_v_na.reshape(DEC_BATCH, PAST_LEN, N_HEADS, HEAD_DIM).transpose(0, 2, 1, 3)
    attn_sample = _na_attention(qkv, cache_k, cache_v, _na_bias_table(na_rpb[0]))
    attn_prompt = _ctx_attention(qkv)
    x = _mm_res([attn_prompt, attn_sample], [na_w_o], 0, [0], [0], D_MODEL, x, mod[1, 2], tm=1024, tn=512,
                split_x=True)
    y_prompt, y_sample = ffn(x, 1, True)

    return (y_prompt.reshape(BATCH, SEQ, D_MODEL), y_sample.reshape(DEC_BATCH, DEC_SEQ, D_MODEL), new_k, new_v)
```
